```python
import math
import jax
import jax.numpy as jnp
from jax import lax
import numpy as np

D_MODEL = 1024
BATCH = 16
SEQ = 256
DEPTH = 2
DEC_BATCH = 4
DEC_SEQ = 2048
PAST_LEN = 256

GRID_W = 64
HEAD_DIM = 64
LRU_WIDTH = 256
LRU_BLOCKS = 4
LRU_BLOCK_W = LRU_WIDTH // LRU_BLOCKS
LRU_CONV_W = 4
LRU_CONV_LEFT = 2
LRU_C = 8.0
GQA_HEADS = 8
GQA_KV_HEADS = 2
GQA_GROUP = GQA_HEADS // GQA_KV_HEADS
GQA_WIDTH = GQA_HEADS * HEAD_DIM
DIFF_HEADS = 4
DIFF_V_DIM = HEAD_DIM
DIFF_QK_DIM = HEAD_DIM // 2
DIFF_WIDTH = DIFF_HEADS * DIFF_V_DIM
MIX_WIDTH = LRU_WIDTH + GQA_WIDTH + DIFF_WIDTH
PROJ_SIZES = [LRU_WIDTH, LRU_WIDTH, GQA_WIDTH, GQA_KV_HEADS * HEAD_DIM, GQA_KV_HEADS * HEAD_DIM,
              DIFF_WIDTH, DIFF_WIDTH, DIFF_WIDTH]
IN_WIDTH = sum(PROJ_SIZES)
D_FF = 2816
FFN_CONV_W = 3
FFN_CONV_LEFT = 1
ROPE_THETA = 10000.0
Q_BLOCK = 128
EPS = 1e-6

kernel_name = 'hybrid_diffusion_parallel_heads_step'


def _rmsnorm(x, g):
    xf = x.astype(jnp.float32)
    y = xf * lax.rsqrt(jnp.mean(xf * xf, axis=-1, keepdims=True) + EPS)
    return (y * g.astype(jnp.float32)).astype(x.dtype)


def _ada(cond, w, b):
    m = jnp.einsum('nd,de->ne', jax.nn.silu(cond), w) + b
    return jnp.split(m[:, None, :], 6, axis=-1)


def _dwconv(x, w, b, left):
    width, ch = w.shape
    y = lax.conv_general_dilated(x, w.reshape(width, 1, ch).astype(x.dtype), window_strides=(1,),
                                 padding=[(left, width - 1 - left)],
                                 dimension_numbers=('NWC', 'WIO', 'NWC'), feature_group_count=ch)
    return y + b


def _rope_2d(S, dim):
    rows = S // GRID_W
    t_row = jnp.repeat(jnp.arange(rows, dtype=jnp.float32), GRID_W)
    t_col = jnp.tile(jnp.arange(GRID_W, dtype=jnp.float32), rows)
    axis_dim = dim // 2
    inv = ROPE_THETA ** (-jnp.arange(0, axis_dim, 2, dtype=jnp.float32) / axis_dim)
    ar = t_row[:, None] * inv
    ac = t_col[:, None] * inv
    ang = jnp.concatenate([ar, ar, ac, ac], axis=-1)
    return jnp.cos(ang), jnp.sin(ang)


def _apply_rope(x, cos, sin):
    xf = x.astype(jnp.float32)
    x1, x2, x3, x4 = jnp.split(xf, 4, axis=-1)
    rot = jnp.concatenate([-x2, x1, -x4, x3], axis=-1)
    return (xf * cos + rot * sin).astype(x.dtype)


def _lin_combine(e1, e2):
    a1, b1 = e1
    a2, b2 = e2
    return a1 * a2, a2 * b1 + b2


def _rglru(x, w, b, lam, h0, reverse):
    B, S, W = x.shape
    g = jnp.einsum('bsnd,nde->bsne', x.reshape(B, S, LRU_BLOCKS, LRU_BLOCK_W), w) + b
    r, i = jnp.split(jax.nn.sigmoid(g.astype(jnp.float32)), 2, axis=-1)
    r = r.reshape(B, S, W)
    i = i.reshape(B, S, W)
    log_a = -LRU_C * r * jax.nn.softplus(-lam.astype(jnp.float32))
    a = jnp.exp(log_a)
    u = jnp.sqrt(-jnp.expm1(2.0 * log_a)) * (i * x.astype(jnp.float32))
    edge = S - 1 if reverse else 0
    u = u.at[:, edge].add(a[:, edge] * h0.astype(jnp.float32))
    _, hs = lax.associative_scan(_lin_combine, (a, u), reverse=reverse, axis=1)
    return hs


def _gqa_attend(q, k, v):
    B, S = q.shape[:2]
    nb = S // Q_BLOCK
    qb = jnp.moveaxis(q.reshape(B, nb, Q_BLOCK, GQA_KV_HEADS, GQA_GROUP, HEAD_DIM), 1, 0)
    scale = HEAD_DIM ** -0.5

    def one(qblk):
        s = jnp.einsum('bqhgd,bkhd->bhgqk', qblk, k).astype(jnp.float32) * scale
        p = jax.nn.softmax(s, axis=-1).astype(v.dtype)
        return jnp.einsum('bhgqk,bkhd->bqhgd', p, v)

    o = lax.map(one, qb)
    return jnp.moveaxis(o, 0, 1).reshape(B, S, GQA_WIDTH)


def _diff_attend(q, k, v, lam):
    B, S = q.shape[:2]
    nb = S // Q_BLOCK
    qb = jnp.moveaxis(q.reshape(B, nb, Q_BLOCK, DIFF_HEADS, 2, DIFF_QK_DIM), 1, 0)
    scale = DIFF_QK_DIM ** -0.5

    def one(qblk):
        s = jnp.einsum('bqhcd,bkhcd->bchqk', qblk, k).astype(jnp.float32) * scale
        p = jax.nn.softmax(s, axis=-1)
        att = (p[:, 0] - lam * p[:, 1]).astype(v.dtype)
        return jnp.einsum('bhqk,bkhd->bqhd', att, v)

    o = lax.map(one, qb)
    return jnp.moveaxis(o, 0, 1).reshape(B, S, DIFF_HEADS, DIFF_V_DIM)


def _mixer(h, p, lam_init, ctx):
    B, S, _ = h.shape
    offs = [int(o) for o in np.cumsum(PROJ_SIZES)[:-1]]
    lru_x, lru_g, gq, gk, gv, dq, dk, dv = jnp.split(jnp.einsum('bsd,de->bse', h, p['w_in']), offs, axis=-1)
    q = _rmsnorm(gq.reshape(B, S, GQA_HEADS, HEAD_DIM), p['q_g']).reshape(B, S, GQA_KV_HEADS, GQA_GROUP, HEAD_DIM)
    k = _rmsnorm(gk.reshape(B, S, GQA_KV_HEADS, HEAD_DIM), p['k_g'])
    v = gv.reshape(B, S, GQA_KV_HEADS, HEAD_DIM)
    dq = dq.reshape(B, S, DIFF_HEADS, 2, DIFF_QK_DIM)
    dk = dk.reshape(B, S, DIFF_HEADS, 2, DIFF_QK_DIM)
    dv = dv.reshape(B, S, DIFF_HEADS, DIFF_V_DIM)
    xc = _dwconv(lru_x, p['lru_conv_w'], p['lru_conv_b'], LRU_CONV_LEFT)
    if ctx is None:
        h0f = jnp.zeros((B, LRU_WIDTH), jnp.float32)
        h0b = h0f
        k_att, v_att, dk_att, dv_att = k, v, dk, dv
    else:
        ck, cv, cdk, cdv, cst = ctx
        cos, sin = _rope_2d(S, HEAD_DIM)
        q = _apply_rope(q, cos[:, None, None], sin[:, None, None])
        k = _apply_rope(k, cos[:, None], sin[:, None])
        cosd, sind = _rope_2d(S, DIFF_QK_DIM)
        dq = _apply_rope(dq, cosd[:, None, None], sind[:, None, None])
        dk = _apply_rope(dk, cosd[:, None, None], sind[:, None, None])
        k_att = jnp.concatenate([ck.astype(k.dtype), k], axis=1)
        v_att = jnp.concatenate([cv.astype(v.dtype), v], axis=1)
        dk_att = jnp.concatenate([cdk.astype(dk.dtype), dk], axis=1)
        dv_att = jnp.concatenate([cdv.astype(dv.dtype), dv], axis=1)
        h0f, h0b = cst[:, 0], cst[:, 1]
    hf = _rglru(xc, p['lru_gate_w'][0], p['lru_gate_b'][0], p['lru_lambda'][0], h0f, False)
    hb = _rglru(xc, p['lru_gate_w'][1], p['lru_gate_b'][1], p['lru_lambda'][1], h0b, True)
    lru_out = jax.nn.gelu(lru_g) * (hf + hb).astype(h.dtype)
    gqa_out = _gqa_attend(q, k_att, v_att)
    lq = p['diff_lambda'].astype(jnp.float32)
    lam = jnp.exp(jnp.sum(lq[0] * lq[1])) - jnp.exp(jnp.sum(lq[2] * lq[3])) + lam_init
    d_out = _rmsnorm(_diff_attend(dq, dk_att, dv_att, lam), p['diff_g']) * (1.0 - lam_init)
    mixed = jnp.concatenate([lru_out, gqa_out, d_out.reshape(B, S, DIFF_WIDTH)], axis=-1)
    out = jnp.einsum('bse,ed->bsd', mixed, p['w_out'])
    if ctx is None:
        state = jnp.stack([hf[:, -1], hb[:, 0]], axis=1).astype(h.dtype)
        return out, (k, v, dk, dv, state)
    return out, None


def _conv_ffn(h, p):
    u = _dwconv(jnp.einsum('bsd,de->bse', h, p['ffn_w_up']), p['ffn_conv_w'], p['ffn_conv_b'], FFN_CONV_LEFT)
    a, g = jnp.split(u, 2, axis=-1)
    return jnp.einsum('bsf,fd->bsd', jax.nn.silu(g) * a, p['ffn_w_down'])


def _layer(x, mod, p, lam_init, ctx):
    sh1, sc1, g1, sh2, sc2, g2 = mod
    h = _rmsnorm(x, p['norm1']) * (1.0 + sc1) + sh1
    m, new_ctx = _mixer(h, p, lam_init, ctx)
    x = x + g1 * m
    h = _rmsnorm(x, p['norm2']) * (1.0 + sc2) + sh2
    x = x + g2 * _conv_ffn(h, p)
    return x, new_ctx


def setup_inputs(seed: int = 0) -> dict:
    key = jax.random.key(seed)
    ks = jax.random.split(key, 32)

    def nrm(k, shape, scale):
        return jax.random.normal(k, shape, jnp.float32) * scale

    s = jax.random.uniform(ks[16], (DEPTH, 2, LRU_WIDTH), jnp.float32, 0.9, 0.999) ** (1.0 / LRU_C)
    return {
        'x_prompt': nrm(ks[0], (BATCH, SEQ, D_MODEL), 1.0),
        'x_sample': nrm(ks[1], (DEC_BATCH, DEC_SEQ, D_MODEL), 1.0),
        'cache_gqa_k': nrm(ks[2], (DEC_BATCH, DEPTH, PAST_LEN, GQA_KV_HEADS, HEAD_DIM), 1.0),
        'cache_gqa_v': nrm(ks[3], (DEC_BATCH, DEPTH, PAST_LEN, GQA_KV_HEADS, HEAD_DIM), 0.5),
        'cache_diff_k': nrm(ks[4], (DEC_BATCH, DEPTH, PAST_LEN, DIFF_HEADS, 2, DIFF_QK_DIM), 0.5),
        'cache_diff_v': nrm(ks[5], (DEC_BATCH, DEPTH, PAST_LEN, DIFF_HEADS, DIFF_V_DIM), 0.5),
        'state_lru': nrm(ks[6], (DEC_BATCH, DEPTH, 2, LRU_WIDTH), 0.5),
        'c': nrm(ks[7], (DEC_BATCH, D_MODEL), 1.0),
        'c_ctx': nrm(ks[8], (D_MODEL,), 1.0),
        'norm1_g': 1.0 + nrm(ks[9], (DEPTH, D_MODEL), 0.05),
        'norm2_g': 1.0 + nrm(ks[10], (DEPTH, D_MODEL), 0.05),
        'final_norm_g': 1.0 + nrm(ks[11], (D_MODEL,), 0.05),
        'ada_w': nrm(ks[12], (DEPTH, D_MODEL, 6 * D_MODEL), D_MODEL ** -0.5),
        'ada_b': nrm(ks[13], (DEPTH, 6 * D_MODEL), 0.02),
        'w_in': nrm(ks[14], (DEPTH, D_MODEL, IN_WIDTH), D_MODEL ** -0.5),
        'w_out': nrm(ks[15], (DEPTH, MIX_WIDTH, D_MODEL), MIX_WIDTH ** -0.5),
        'lru_conv_w': nrm(ks[17], (DEPTH, LRU_CONV_W, LRU_WIDTH), LRU_CONV_W ** -0.5),
        'lru_conv_b': nrm(ks[18], (DEPTH, LRU_WIDTH), 0.02),
        'lru_gate_w': nrm(ks[19], (DEPTH, 2, LRU_BLOCKS, LRU_BLOCK_W, 2 * LRU_BLOCK_W), LRU_BLOCK_W ** -0.5),
        'lru_gate_b': nrm(ks[20], (DEPTH, 2, LRU_BLOCKS, 2 * LRU_BLOCK_W), 0.02),
        'lru_lambda': jnp.log(s) - jnp.log1p(-s),
        'gqa_q_norm_g': 1.0 + nrm(ks[21], (DEPTH, HEAD_DIM), 0.05),
        'gqa_k_norm_g': 1.0 + nrm(ks[22], (DEPTH, HEAD_DIM), 0.05),
        'diff_lambda': nrm(ks[23], (DEPTH, 4, DIFF_QK_DIM), 0.1),
        'diff_norm_g': 1.0 + nrm(ks[24], (DEPTH, DIFF_V_DIM), 0.05),
        'ffn_w_up': nrm(ks[25], (DEPTH, D_MODEL, 2 * D_FF), D_MODEL ** -0.5),
        'ffn_conv_w': nrm(ks[26], (DEPTH, FFN_CONV_W, 2 * D_FF), FFN_CONV_W ** -0.5),
        'ffn_conv_b': nrm(ks[27], (DEPTH, 2 * D_FF), 0.02),
        'ffn_w_down': nrm(ks[28], (DEPTH, D_FF, D_MODEL), D_FF ** -0.5),
    }


def reference(x_prompt, x_sample, cache_gqa_k, cache_gqa_v, cache_diff_k, cache_diff_v, state_lru, c,
              c_ctx, norm1_g, norm2_g, final_norm_g, ada_w, ada_b, w_in, w_out, lru_conv_w, lru_conv_b,
              lru_gate_w, lru_gate_b, lru_lambda, gqa_q_norm_g, gqa_k_norm_g, diff_lambda, diff_norm_g,
              ffn_w_up, ffn_conv_w, ffn_conv_b, ffn_w_down):
    xp = x_prompt
    xs = x_sample
    ks, vs, dks, dvs, sts = [], [], [], [], []
    for l in range(DEPTH):
        p = {'norm1': norm1_g[l], 'norm2': norm2_g[l], 'w_in': w_in[l], 'w_out': w_out[l],
             'lru_conv_w': lru_conv_w[l], 'lru_conv_b': lru_conv_b[l], 'lru_gate_w': lru_gate_w[l],
             'lru_gate_b': lru_gate_b[l], 'lru_lambda': lru_lambda[l], 'q_g': gqa_q_norm_g[l],
             'k_g': gqa_k_norm_g[l], 'diff_lambda': diff_lambda[l], 'diff_g': diff_norm_g[l],
             'ffn_w_up': ffn_w_up[l], 'ffn_conv_w': ffn_conv_w[l], 'ffn_conv_b': ffn_conv_b[l],
             'ffn_w_down': ffn_w_down[l]}
        lam_init = 0.8 - 0.6 * math.exp(-0.3 * l)
        xp, (k_l, v_l, dk_l, dv_l, st_l) = _layer(xp, _ada(c_ctx[None], ada_w[l], ada_b[l]), p, lam_init, None)
        ks.append(k_l)
        vs.append(v_l)
        dks.append(dk_l)
        dvs.append(dv_l)
        sts.append(st_l)
        ctx = (cache_gqa_k[:, l], cache_gqa_v[:, l], cache_diff_k[:, l], cache_diff_v[:, l], state_lru[:, l])
        xs, _ = _layer(xs, _ada(c, ada_w[l], ada_b[l]), p, lam_init, ctx)
    y_prompt = _rmsnorm(xp, final_norm_g)
    y_sample = _rmsnorm(xs, final_norm_g)
    return (y_prompt, y_sample, jnp.stack(ks, axis=1), jnp.stack(vs, axis=1), jnp.stack(dks, axis=1),
            jnp.stack(dvs, axis=1), jnp.stack(sts, axis=1))
```

```python
import functools
import math

import numpy as np
import jax
import jax.numpy as jnp
from jax import lax
from jax.experimental import pallas as pl
from jax.experimental.pallas import tpu as pltpu

F32 = jnp.float32
BF16 = jnp.bfloat16

D_MODEL = 1024
DEPTH = 2
GRID_W = 64
HEAD_DIM = 64
LRU_WIDTH = 256
LRU_BLOCKS = 4
LRU_BLOCK_W = LRU_WIDTH // LRU_BLOCKS
LRU_C = 8.0
GQA_HEADS = 8
GQA_KV_HEADS = 2
DIFF_HEADS = 4
DIFF_QK_DIM = 32
D_FF = 2816
ROPE_THETA = 10000.0
EPS = 1e-6

LANES = 128
FFN_CHUNK = 256
N_FFN_CHUNKS = D_FF // FFN_CHUNK
COND_ROWS = 8
VMEM_LIMIT = 56 * 2 ** 20


def _params(*sem):
    return pltpu.CompilerParams(dimension_semantics=sem, vmem_limit_bytes=VMEM_LIMIT)


def _rms(x):
    return x * lax.rsqrt(jnp.mean(x * x, axis=-1, keepdims=True) + EPS)


def _ada_kernel(cond_ref, w_ref, b_ref, o_ref):
    c = cond_ref[...]
    s = c * jax.nn.sigmoid(c)
    o_ref[...] = jnp.dot(s.astype(BF16), w_ref[...].astype(BF16), preferred_element_type=F32) + b_ref[...]


def _ada(cond, ada_w, ada_b):
    tn = 1536
    width = 6 * D_MODEL
    return pl.pallas_call(
        _ada_kernel,
        grid=(DEPTH, width // tn),
        in_specs=[pl.BlockSpec((COND_ROWS, D_MODEL), lambda l, j: (0, 0)),
                  pl.BlockSpec((None, D_MODEL, tn), lambda l, j: (l, 0, j)),
                  pl.BlockSpec((None, 1, tn), lambda l, j: (l, 0, j))],
        out_specs=pl.BlockSpec((None, COND_ROWS, tn), lambda l, j: (l, 0, j)),
        out_shape=jax.ShapeDtypeStruct((DEPTH, COND_ROWS, width), F32),
        compiler_params=_params("arbitrary", "arbitrary"),
        name="ada",
    )(cond, ada_w, ada_b.reshape(DEPTH, 1, width))


def _group_mean_sq(x, ones_blk):
    sq = x * x
    hi = sq.astype(BF16)
    lo = (sq - hi.astype(F32)).astype(BF16)
    s = jnp.dot(hi, ones_blk, preferred_element_type=F32) + jnp.dot(lo, ones_blk, preferred_element_type=F32)
    return s * (1.0 / HEAD_DIM)


def _rope(x, cos, sin_neg, sin_pos, quarter):
    return x * cos + pltpu.roll(x, LANES - quarter, 1) * sin_neg + pltpu.roll(x, quarter, 1) * sin_pos


def _norm_proj_kernel(*refs, rope):
    if rope:
        (x_ref, mod_ref, n1_ref, w_ref, qg_ref, kg_ref, ones_ref, cq, snq, spq, cd, snd, spd,
         lru_ref, qx_ref, k_ref, v_ref, dqx_ref, dk_ref, dv_ref) = refs
    else:
        (x_ref, mod_ref, n1_ref, w_ref, qg_ref, kg_ref, ones_ref,
         lru_ref, qx_ref, k_ref, v_ref, dqx_ref, dk_ref, dv_ref) = refs
    mod = mod_ref[...]
    h = _rms(x_ref[...]) * (n1_ref[...] * (1.0 + mod[:, D_MODEL:2 * D_MODEL])) + mod[:, 0:D_MODEL]
    proj = jnp.dot(h.astype(BF16), w_ref[...], preferred_element_type=F32)
    lru_ref[...] = proj[:, 0:512]
    ones_blk = ones_ref[...]
    lane = lax.broadcasted_iota(jnp.int32, (1, LANES), 1)

    for c in range(4):
        xc = proj[:, 512 + LANES * c:512 + LANES * (c + 1)]
        xc = xc * lax.rsqrt(_group_mean_sq(xc, ones_blk) + EPS) * qg_ref[...]
        if rope:
            xc = _rope(xc, cq[...], snq[...], spq[...], HEAD_DIM // 4)
        xc = xc * (HEAD_DIM ** -0.5)
        xr = pltpu.roll(xc, HEAD_DIM, 1)
        for par in range(2):
            j = 2 * c + par
            want = j // (GQA_HEADS // GQA_KV_HEADS)
            src = xc if par == want else xr
            qx_ref[j] = jnp.where(lane // HEAD_DIM == want, src, 0.0).astype(BF16)

    kc = proj[:, 1024:1152]
    kc = kc * lax.rsqrt(_group_mean_sq(kc, ones_blk) + EPS) * kg_ref[...]
    if rope:
        kc = _rope(kc, cq[...], snq[...], spq[...], HEAD_DIM // 4)
    k_ref[...] = kc
    v_ref[...] = proj[:, 1152:1280]

    for c in range(2):
        xc = proj[:, 1280 + LANES * c:1280 + LANES * (c + 1)]
        if rope:
            xc = _rope(xc, cd[...], snd[...], spd[...], DIFF_QK_DIM // 4)
        for g in range(4):
            dqx_ref[4 * c + g] = jnp.where(lane // DIFF_QK_DIM == g, xc, 0.0).astype(BF16)
        kc = proj[:, 1536 + LANES * c:1536 + LANES * (c + 1)]
        if rope:
            kc = _rope(kc, cd[...], snd[...], spd[...], DIFF_QK_DIM // 4)
        dk_ref[:, LANES * c:LANES * (c + 1)] = kc
    dv_ref[...] = proj[:, 1792:2048]


def _norm_proj(x, mod, n1, w_in, qg, kg, ones_blk, tables, *, seq, tm, mod_row):
    n = x.shape[0]
    rope = tables is not None
    nts = seq // tm
    row = lambda i: (i, 0)
    const = lambda i: (0, 0)
    in_specs = [pl.BlockSpec((tm, D_MODEL), row),
                pl.BlockSpec((None, 1, 6 * D_MODEL), lambda i: (mod_row(i // nts), 0, 0)),
                pl.BlockSpec((1, D_MODEL), const),
                pl.BlockSpec((D_MODEL, 2048), const),
                pl.BlockSpec((1, LANES), const),
                pl.BlockSpec((1, LANES), const),
                pl.BlockSpec((LANES, LANES), const)]
    args = [x, mod, n1, w_in, qg, kg, ones_blk]
    if rope:
        in_specs += [pl.BlockSpec((tm, LANES), lambda i: (i % nts, 0))] * 6
        args += list(tables)
    slab = lambda i: (0, i, 0)
    out_specs = [pl.BlockSpec((tm, 512), row),
                 pl.BlockSpec((8, tm, LANES), slab),
                 pl.BlockSpec((tm, 128), row),
                 pl.BlockSpec((tm, 128), row),
                 pl.BlockSpec((8, tm, LANES), slab),
                 pl.BlockSpec((tm, 256), row),
                 pl.BlockSpec((tm, 256), row)]
    out_shape = [jax.ShapeDtypeStruct((n, 512), F32),
                 jax.ShapeDtypeStruct((8, n, LANES), BF16),
                 jax.ShapeDtypeStruct((n, 128), F32),
                 jax.ShapeDtypeStruct((n, 128), F32),
                 jax.ShapeDtypeStruct((8, n, LANES), BF16),
                 jax.ShapeDtypeStruct((n, 256), F32),
                 jax.ShapeDtypeStruct((n, 256), F32)]
    return pl.pallas_call(
        functools.partial(_norm_proj_kernel, rope=rope),
        grid=(n // tm,), in_specs=in_specs, out_specs=out_specs, out_shape=out_shape,
        compiler_params=_params("arbitrary"), name="norm_proj",
    )(*args)


def _chunk_scan(a, u, row, reverse):
    for d in (1, 2, 4):
        shift = 8 - d if reverse else d
        a_s = pltpu.roll(a, shift, 0)
        u_s = pltpu.roll(u, shift, 0)
        m = (row < 8 - d) if reverse else (row >= d)
        u = jnp.where(m, a * u_s + u, u)
        a = jnp.where(m, a * a_s, a)
    return a, u


def _lru_kernel(x_ref, xp_ref, xn_ref, cw_ref, cb_ref, gw_ref, gb_ref, lam_ref, h0_ref, out_ref, st_ref,
                af, uf, ab, ub, gg, *, seq, tr):
    j = pl.program_id(1)
    nt = seq // tr
    xg = x_ref[...]
    x = xg[:, 0:LRU_WIDTH]
    prev = jnp.where(j > 0, xp_ref[...], 0.0)
    nxt = jnp.where(j < nt - 1, xn_ref[...], 0.0)
    xe = jnp.concatenate([prev, x, nxt], axis=0)
    ne = tr + 16
    cw = cw_ref[...]
    xc = (cw[0:1] * pltpu.roll(xe, 2, 0)[8:8 + tr] + cw[1:2] * pltpu.roll(xe, 1, 0)[8:8 + tr]
          + cw[2:3] * x + cw[3:4] * pltpu.roll(xe, ne - 1, 0)[8:8 + tr] + cb_ref[...])
    sg = jax.nn.sigmoid(jnp.dot(xc.astype(BF16), gw_ref[...], preferred_element_type=F32) + gb_ref[...])
    z = -lam_ref[...]
    neg_c_softplus = -LRU_C * (jnp.maximum(z, 0.0) + jnp.log1p(jnp.exp(-jnp.abs(z))))
    r0 = pl.multiple_of(j * tr, tr)
    for d, (a_s, u_s) in enumerate(((af, uf), (ab, ub))):
        r = sg[:, 512 * d:512 * d + LRU_WIDTH]
        i = sg[:, 512 * d + LRU_WIDTH:512 * (d + 1)]
        log_a = neg_c_softplus[d:d + 1] * r
        a = jnp.exp(log_a)
        a_s[pl.ds(r0, tr), :] = a
        u_s[pl.ds(r0, tr), :] = jnp.sqrt(-jnp.tanh(log_a) * (a * a + 1.0)) * (i * xc)
    gg[pl.ds(r0, tr), :] = jax.nn.gelu(xg[:, LRU_WIDTH:2 * LRU_WIDTH])

    @pl.when(j == nt - 1)
    def _():
        row = lax.broadcasted_iota(jnp.int32, (8, LRU_WIDTH), 0)
        nchunk = seq // 8

        def body(c, carry):
            hf, hb = carry
            rf = pl.multiple_of(c * 8, 8)
            a, u = _chunk_scan(af[pl.ds(rf, 8), :], uf[pl.ds(rf, 8), :], row, False)
            hs = u + a * hf
            uf[pl.ds(rf, 8), :] = hs
            rb = pl.multiple_of((nchunk - 1 - c) * 8, 8)
            a2, u2 = _chunk_scan(ab[pl.ds(rb, 8), :], ub[pl.ds(rb, 8), :], row, True)
            hs2 = u2 + a2 * hb
            ub[pl.ds(rb, 8), :] = hs2
            return hs[7:8, :], hs2[0:1, :]

        h0 = h0_ref[...]
        hf, hb = lax.fori_loop(0, nchunk, body, (h0[0:1], h0[1:2]))
        st_ref[...] = jnp.concatenate([hf, hb], axis=0)

        def obody(t, carry):
            r = pl.multiple_of(t * tr, tr)
            out_ref[pl.ds(r, tr), :] = (gg[pl.ds(r, tr), :] * (uf[pl.ds(r, tr), :] + ub[pl.ds(r, tr), :])).astype(BF16)
            return carry

        lax.fori_loop(0, nt, obody, 0)


def _lru(lru, cw, cb, gw, gb, lam, h0, *, seq, tr):
    n = lru.shape[0]
    b = n // seq
    nt = seq // tr
    nb8 = n // 8
    const = lambda bi, j: (0, 0)
    in_specs = [pl.BlockSpec((tr, 512), lambda bi, j: (bi * nt + j, 0)),
                pl.BlockSpec((8, LRU_WIDTH), lambda bi, j: (jnp.maximum((bi * nt + j) * (tr // 8) - 1, 0), 0)),
                pl.BlockSpec((8, LRU_WIDTH), lambda bi, j: (jnp.minimum((bi * nt + j + 1) * (tr // 8), nb8 - 1), 0)),
                pl.BlockSpec((4, LRU_WIDTH), const),
                pl.BlockSpec((1, LRU_WIDTH), const),
                pl.BlockSpec((LRU_WIDTH, 4 * LRU_WIDTH), const),
                pl.BlockSpec((1, 4 * LRU_WIDTH), const),
                pl.BlockSpec((2, LRU_WIDTH), const),
                pl.BlockSpec((None, 2, LRU_WIDTH), lambda bi, j: (bi, 0, 0))]
    out_specs = [pl.BlockSpec((seq, LRU_WIDTH), lambda bi, j: (bi, 0)),
                 pl.BlockSpec((None, 2, LRU_WIDTH), lambda bi, j: (bi, 0, 0))]
    out_shape = [jax.ShapeDtypeStruct((n, LRU_WIDTH), BF16),
                 jax.ShapeDtypeStruct((b, 2, LRU_WIDTH), F32)]
    return pl.pallas_call(
        functools.partial(_lru_kernel, seq=seq, tr=tr),
        grid=(b, nt), in_specs=in_specs, out_specs=out_specs, out_shape=out_shape,
        scratch_shapes=[pltpu.VMEM((seq, LRU_WIDTH), F32)] * 5,
        compiler_params=_params("arbitrary", "arbitrary"), name="lru",
    )(lru, lru, lru, cw, cb, gw, gb, lam, h0)


_NT = (((1,), (1,)), ((), ()))


def _attn_kernel(*refs, n_ctx, seq, tq, lam_init):
    if n_ctx:
        (qx_ref, dqx_ref, k_ref, v_ref, dk_ref, dv_ref, ck_ref, cv_ref, cdk_ref, cdv_ref, dl_ref, dg_ref,
         gqa_ref, dout_ref, kb, vb, dkb, dvb, og, od) = refs
    else:
        (qx_ref, dqx_ref, k_ref, v_ref, dk_ref, dv_ref, dl_ref, dg_ref,
         gqa_ref, dout_ref, kb, vb, dkb, dvb, og, od) = refs
    t_all = n_ctx + seq

    @pl.when(pl.program_id(1) == 0)
    def _():
        if n_ctx:
            kb[0:n_ctx, :] = ck_ref[...].astype(BF16)
            vb[0:n_ctx, :] = cv_ref[...].astype(BF16)
        kb[n_ctx:t_all, :] = k_ref[...].astype(BF16)
        vb[n_ctx:t_all, :] = v_ref[...].astype(BF16)
        for c in range(2):
            if n_ctx:
                dkb[c, 0:n_ctx, :] = cdk_ref[:, LANES * c:LANES * (c + 1)].astype(BF16)
                dvb[c, 0:n_ctx, :] = cdv_ref[:, LANES * c:LANES * (c + 1)].astype(BF16)
            dkb[c, n_ctx:t_all, :] = dk_ref[:, LANES * c:LANES * (c + 1)].astype(BF16)
            dvb[c, n_ctx:t_all, :] = dv_ref[:, LANES * c:LANES * (c + 1)].astype(BF16)

    lane = lax.broadcasted_iota(jnp.int32, (1, LANES), 1)
    low_half = lane < HEAD_DIM

    def gqa_body(j, carry):
        s = lax.dot_general(qx_ref[j], kb[...], _NT, preferred_element_type=F32)
        e = jnp.exp(s - jnp.max(s, axis=-1, keepdims=True))
        l = jnp.sum(e, axis=-1, keepdims=True)
        og[j] = jnp.dot(e.astype(BF16), vb[...], preferred_element_type=F32) * (1.0 / l)
        return carry

    lax.fori_loop(0, GQA_HEADS, gqa_body, 0)
    for c in range(4):
        want = c // 2
        parts = []
        for par in range(2):
            o = og[2 * c + par]
            parts.append(o if par == want else pltpu.roll(o, HEAD_DIM, 1))
        gqa_ref[:, LANES * c:LANES * (c + 1)] = jnp.where(low_half, parts[0], parts[1]).astype(BF16)

    dl = dl_ref[...]
    lam = (jnp.exp(jnp.sum(dl[0:1] * dl[1:2], axis=-1, keepdims=True))
           - jnp.exp(jnp.sum(dl[2:3] * dl[3:4], axis=-1, keepdims=True)) + lam_init)
    dscale = DIFF_QK_DIM ** -0.5

    def diff_body(h, carry):
        kc = dkb[h // 2]
        s1 = lax.dot_general(dqx_ref[2 * h], kc, _NT, preferred_element_type=F32) * dscale
        s2 = lax.dot_general(dqx_ref[2 * h + 1], kc, _NT, preferred_element_type=F32) * dscale
        e1 = jnp.exp(s1 - jnp.max(s1, axis=-1, keepdims=True))
        e2 = jnp.exp(s2 - jnp.max(s2, axis=-1, keepdims=True))
        l1 = jnp.sum(e1, axis=-1, keepdims=True)
        l2 = jnp.sum(e2, axis=-1, keepdims=True)
        inv1 = 1.0 / l1
        att = (e1 - (lam * l1 / l2) * e2).astype(BF16)
        od[h] = jnp.dot(att, dvb[h // 2], preferred_element_type=F32) * inv1
        return carry

    lax.fori_loop(0, DIFF_HEADS, diff_body, 0)
    for c in range(2):
        o = jnp.where(low_half, od[2 * c], od[2 * c + 1])
        sq = o * o
        ms = jnp.where(low_half,
                       jnp.sum(jnp.where(low_half, sq, 0.0), axis=-1, keepdims=True),
                       jnp.sum(jnp.where(low_half, 0.0, sq), axis=-1, keepdims=True)) * (1.0 / HEAD_DIM)
        dout_ref[:, LANES * c:LANES * (c + 1)] = (
            o * lax.rsqrt(ms + EPS) * dg_ref[...] * (1.0 - lam_init)).astype(BF16)


def _attention(qx, dqx, k, v, dk, dv, caches, dl, dg, *, layer, seq, tq, lam_init):
    n = k.shape[0]
    b = n // seq
    nq = seq // tq
    n_ctx = 0 if caches is None else caches[0].shape[2]
    t_all = n_ctx + seq
    slab = lambda bi, i: (0, bi * nq + i, 0)
    per_b = lambda bi, i: (bi, 0)
    const = lambda bi, i: (0, 0)
    in_specs = [pl.BlockSpec((8, tq, LANES), slab),
                pl.BlockSpec((8, tq, LANES), slab),
                pl.BlockSpec((seq, 128), per_b),
                pl.BlockSpec((seq, 128), per_b),
                pl.BlockSpec((seq, 256), per_b),
                pl.BlockSpec((seq, 256), per_b)]
    args = [qx, dqx, k, v, dk, dv]
    if n_ctx:
        cache_idx = lambda bi, i: (bi, layer, 0, 0)
        in_specs += [pl.BlockSpec((None, None, n_ctx, 128), cache_idx),
                     pl.BlockSpec((None, None, n_ctx, 128), cache_idx),
                     pl.BlockSpec((None, None, n_ctx, 256), cache_idx),
                     pl.BlockSpec((None, None, n_ctx, 256), cache_idx)]
        args += list(caches)
    in_specs += [pl.BlockSpec((4, DIFF_QK_DIM), const), pl.BlockSpec((1, LANES), const)]
    args += [dl, dg]
    row = lambda bi, i: (bi * nq + i, 0)
    return pl.pallas_call(
        functools.partial(_attn_kernel, n_ctx=n_ctx, seq=seq, tq=tq, lam_init=lam_init),
        grid=(b, nq), in_specs=in_specs,
        out_specs=[pl.BlockSpec((tq, 512), row), pl.BlockSpec((tq, 256), row)],
        out_shape=[jax.ShapeDtypeStruct((n, 512), BF16), jax.ShapeDtypeStruct((n, 256), BF16)],
        scratch_shapes=[pltpu.VMEM((t_all, LANES), BF16), pltpu.VMEM((t_all, LANES), BF16),
                        pltpu.VMEM((2, t_all, LANES), BF16), pltpu.VMEM((2, t_all, LANES), BF16),
                        pltpu.VMEM((GQA_HEADS, tq, LANES), F32), pltpu.VMEM((DIFF_HEADS, tq, LANES), F32)],
        compiler_params=_params("arbitrary", "arbitrary"), name="attention",
    )(*args)


def _out_proj_kernel(lru_ref, gqa_ref, d_ref, w_ref, x_ref, mod_ref, n2_ref, x1_ref, h2_ref):
    m = (jnp.dot(lru_ref[...], w_ref[0:256, :], preferred_element_type=F32)
         + jnp.dot(gqa_ref[...], w_ref[256:768, :], preferred_element_type=F32)
         + jnp.dot(d_ref[...], w_ref[768:1024, :], preferred_element_type=F32))
    mod = mod_ref[...]
    x1 = x_ref[...] + mod[:, 2 * D_MODEL:3 * D_MODEL] * m
    x1_ref[...] = x1
    h2_ref[...] = (_rms(x1) * (n2_ref[...] * (1.0 + mod[:, 4 * D_MODEL:5 * D_MODEL]))
                   + mod[:, 3 * D_MODEL:4 * D_MODEL]).astype(BF16)


def _out_proj(lru_o, gqa_o, d_o, w_out, x, mod, n2, *, seq, tm, mod_row):
    n = x.shape[0]
    nts = seq // tm
    row = lambda i: (i, 0)
    const = lambda i: (0, 0)
    return pl.pallas_call(
        _out_proj_kernel,
        grid=(n // tm,),
        in_specs=[pl.BlockSpec((tm, 256), row), pl.BlockSpec((tm, 512), row), pl.BlockSpec((tm, 256), row),
                  pl.BlockSpec((D_MODEL, D_MODEL), const),
                  pl.BlockSpec((tm, D_MODEL), row),
                  pl.BlockSpec((None, 1, 6 * D_MODEL), lambda i: (mod_row(i // nts), 0, 0)),
                  pl.BlockSpec((1, D_MODEL), const)],
        out_specs=[pl.BlockSpec((tm, D_MODEL), row), pl.BlockSpec((tm, D_MODEL), row)],
        out_shape=[jax.ShapeDtypeStruct((n, D_MODEL), F32), jax.ShapeDtypeStruct((n, D_MODEL), BF16)],
        compiler_params=_params("arbitrary"), name="out_proj",
    )(lru_o, gqa_o, d_o, w_out, x, mod, n2)


HALO = 16


def _ffn_kernel(h_ref, hp_ref, hn_ref, x1_ref, wa_ref, wg_ref, cwa_ref, cba_ref, cwg_ref, cbg_ref, wd_ref,
                mod_ref, fin_ref, o_ref, he, acc, *, seq, tm, final):
    i = pl.program_id(0)
    nts = seq // tm
    pos = i % nts

    @pl.when(pos == 0)
    def _():
        he[0:HALO, :] = jnp.zeros((HALO, D_MODEL), BF16)

    @pl.when(pos != 0)
    def _():
        he[0:HALO, :] = hp_ref[...]

    @pl.when(pos == nts - 1)
    def _():
        he[HALO + tm:2 * HALO + tm, :] = jnp.zeros((HALO, D_MODEL), BF16)

    @pl.when(pos != nts - 1)
    def _():
        he[HALO + tm:2 * HALO + tm, :] = hn_ref[...]

    he[HALO:HALO + tm, :] = h_ref[...]
    acc[...] = jnp.zeros((tm, D_MODEL), F32)
    ne = tm + 2 * HALO

    def conv(u, cw, cb):
        return (cw[0:1] * pltpu.roll(u, 1, 0)[HALO:HALO + tm] + cw[1:2] * u[HALO:HALO + tm]
                + cw[2:3] * pltpu.roll(u, ne - 1, 0)[HALO:HALO + tm] + cb)

    def body(c, carry):
        lhs = he[...]
        a = conv(jnp.dot(lhs, wa_ref[c], preferred_element_type=F32), cwa_ref[c], cba_ref[c])
        g = conv(jnp.dot(lhs, wg_ref[c], preferred_element_type=F32), cwg_ref[c], cbg_ref[c])
        act = (g * jax.nn.sigmoid(g) * a).astype(BF16)
        acc[...] += jnp.dot(act, wd_ref[c], preferred_element_type=F32)
        return carry

    lax.fori_loop(0, N_FFN_CHUNKS, body, 0)
    mod = mod_ref[...]
    x2 = x1_ref[...] + mod[:, 5 * D_MODEL:6 * D_MODEL] * acc[...]
    if final:
        x2 = _rms(x2) * fin_ref[...]
    o_ref[...] = x2


def _conv_ffn(h2, x1, wa, wg, cwa, cba, cwg, cbg, wd, mod, fin_g, *, seq, tm, mod_row, final):
    n = x1.shape[0]
    nts = seq // tm
    nbh = n // HALO
    row = lambda i: (i, 0)
    c2 = lambda i: (0, 0)
    c3 = lambda i: (0, 0, 0)
    in_specs = [pl.BlockSpec((tm, D_MODEL), row),
                pl.BlockSpec((HALO, D_MODEL), lambda i: (jnp.maximum(i * (tm // HALO) - 1, 0), 0)),
                pl.BlockSpec((HALO, D_MODEL), lambda i: (jnp.minimum((i + 1) * (tm // HALO), nbh - 1), 0)),
                pl.BlockSpec((tm, D_MODEL), row),
                pl.BlockSpec((N_FFN_CHUNKS, D_MODEL, FFN_CHUNK), c3),
                pl.BlockSpec((N_FFN_CHUNKS, D_MODEL, FFN_CHUNK), c3),
                pl.BlockSpec((N_FFN_CHUNKS, 3, FFN_CHUNK), c3),
                pl.BlockSpec((N_FFN_CHUNKS, 1, FFN_CHUNK), c3),
                pl.BlockSpec((N_FFN_CHUNKS, 3, FFN_CHUNK), c3),
                pl.BlockSpec((N_FFN_CHUNKS, 1, FFN_CHUNK), c3),
                pl.BlockSpec((N_FFN_CHUNKS, FFN_CHUNK, D_MODEL), c3),
                pl.BlockSpec((None, 1, 6 * D_MODEL), lambda i: (mod_row(i // nts), 0, 0)),
                pl.BlockSpec((1, D_MODEL), c2)]
    return pl.pallas_call(
        functools.partial(_ffn_kernel, seq=seq, tm=tm, final=final),
        grid=(n // tm,), in_specs=in_specs,
        out_specs=pl.BlockSpec((tm, D_MODEL), row),
        out_shape=jax.ShapeDtypeStruct((n, D_MODEL), F32),
        scratch_shapes=[pltpu.VMEM((tm + 2 * HALO, D_MODEL), BF16), pltpu.VMEM((tm, D_MODEL), F32)],
        compiler_params=_params("arbitrary"), name="conv_ffn",
    )(h2, h2, h2, x1, wa, wg, cwa, cba, cwg, cbg, wd, mod, fin_g)


def _rope_tables(seq, dim):
    rows = seq // GRID_W
    t_row = np.repeat(np.arange(rows, dtype=np.float64), GRID_W)
    t_col = np.tile(np.arange(GRID_W, dtype=np.float64), rows)
    axis_dim = dim // 2
    inv = ROPE_THETA ** (-np.arange(0, axis_dim, 2, dtype=np.float64) / axis_dim)
    ar = t_row[:, None] * inv
    ac = t_col[:, None] * inv
    ang = np.concatenate([ar, ar, ac, ac], axis=-1)
    reps = LANES // dim
    cos = np.tile(np.cos(ang), (1, reps))
    sin = np.tile(np.sin(ang), (1, reps))
    first = (np.arange(LANES) % (dim // 2)) < (dim // 4)
    sin_neg = np.where(first, -sin, 0.0)
    sin_pos = np.where(first, 0.0, sin)
    return tuple(jnp.asarray(t, F32) for t in (cos, sin_neg, sin_pos))


def _gate_weights(w, b):
    eye = jnp.eye(LRU_BLOCKS, dtype=w.dtype)
    cols, biases = [], []
    for d in range(2):
        for part in range(2):
            blk = w[d, :, :, part * LRU_BLOCK_W:(part + 1) * LRU_BLOCK_W]
            cols.append(jnp.einsum('nde,nm->ndme', blk, eye).reshape(LRU_WIDTH, LRU_WIDTH))
            biases.append(b[d, :, part * LRU_BLOCK_W:(part + 1) * LRU_BLOCK_W].reshape(LRU_WIDTH))
    return jnp.concatenate(cols, axis=1).astype(BF16), jnp.concatenate(biases)[None, :]


def _chunked_cols(w):
    return jnp.transpose(w.reshape(w.shape[0], N_FFN_CHUNKS, FFN_CHUNK), (1, 0, 2))


def kernel(x_prompt, x_sample, cache_gqa_k, cache_gqa_v, cache_diff_k, cache_diff_v, state_lru, c, c_ctx,
           norm1_g, norm2_g, final_norm_g, ada_w, ada_b, w_in, w_out, lru_conv_w, lru_conv_b, lru_gate_w,
           lru_gate_b, lru_lambda, gqa_q_norm_g, gqa_k_norm_g, diff_lambda, diff_norm_g, ffn_w_up, ffn_conv_w,
           ffn_conv_b, ffn_w_down):
    bp, sp, _ = x_prompt.shape
    bs, ss, _ = x_sample.shape
    n_ctx = cache_gqa_k.shape[2]

    cond = jnp.concatenate([c_ctx[None, :], c, jnp.zeros((COND_ROWS - 1 - bs, D_MODEL), F32)], axis=0)
    mod_all = _ada(cond, ada_w, ada_b)

    caches = (cache_gqa_k.reshape(bs, DEPTH, n_ctx, 128), cache_gqa_v.reshape(bs, DEPTH, n_ctx, 128),
              cache_diff_k.reshape(bs, DEPTH, n_ctx, 256), cache_diff_v.reshape(bs, DEPTH, n_ctx, 256))
    tables = _rope_tables(ss, HEAD_DIM) + _rope_tables(ss, DIFF_QK_DIM)
    ones_blk = jnp.asarray(np.kron(np.eye(2), np.ones((HEAD_DIM, HEAD_DIM))), BF16)
    zero_state = jnp.zeros((bp, 2, LRU_WIDTH), F32)

    groups = {
        'p': dict(seq=sp, tm=256, tq=256, mod_row=lambda b: 0, tables=None, caches=None),
        's': dict(seq=ss, tm=512, tq=256, mod_row=lambda b: 1 + b, tables=tables, caches=caches),
    }
    xs = {'p': x_prompt.reshape(bp * sp, D_MODEL), 's': x_sample.reshape(bs * ss, D_MODEL)}
    new_k, new_v, new_dk, new_dv, new_st = [], [], [], [], []

    for l in range(DEPTH):
        lam_init = 0.8 - 0.6 * math.exp(-0.3 * l)
        mod = mod_all[l].reshape(COND_ROWS, 1, 6 * D_MODEL)
        w_in_l = w_in[l].astype(BF16)
        w_out_l = w_out[l].astype(BF16)
        gw, gb = _gate_weights(lru_gate_w[l], lru_gate_b[l])
        qg = jnp.tile(gqa_q_norm_g[l], 2)[None, :]
        kg = jnp.tile(gqa_k_norm_g[l], 2)[None, :]
        dg = jnp.tile(diff_norm_g[l], 2)[None, :]
        up = ffn_w_up[l].astype(BF16)
        wa, wg = _chunked_cols(up[:, :D_FF]), _chunked_cols(up[:, D_FF:])
        cwa, cwg = _chunked_cols(ffn_conv_w[l][:, :D_FF]), _chunked_cols(ffn_conv_w[l][:, D_FF:])
        cba, cbg = _chunked_cols(ffn_conv_b[l][None, :D_FF]), _chunked_cols(ffn_conv_b[l][None, D_FF:])
        wd = ffn_w_down[l].astype(BF16).reshape(N_FFN_CHUNKS, FFN_CHUNK, D_MODEL)
        n1 = norm1_g[l][None, :]
        n2 = norm2_g[l][None, :]
        fin = final_norm_g[None, :]

        for name in ('p', 's'):
            g = groups[name]
            seq, tm, mod_row = g['seq'], g['tm'], g['mod_row']
            x = xs[name]
            lru, qx, k, v, dqx, dk, dv = _norm_proj(x, mod, n1, w_in_l, qg, kg, ones_blk, g['tables'],
                                                    seq=seq, tm=tm, mod_row=mod_row)
            h0 = zero_state if name == 'p' else state_lru[:, l]
            lru_o, st = _lru(lru, lru_conv_w[l], lru_conv_b[l][None, :], gw, gb, lru_lambda[l], h0, seq=seq, tr=256)
            gqa_o, d_o = _attention(qx, dqx, k, v, dk, dv, g['caches'], diff_lambda[l], dg,
                                    layer=l, seq=seq, tq=g['tq'], lam_init=lam_init)
            x1, h2 = _out_proj(lru_o, gqa_o, d_o, w_out_l, x, mod, n2, seq=seq, tm=tm, mod_row=mod_row)
            xs[name] = _conv_ffn(h2, x1, wa, wg, cwa, cba, cwg, cbg, wd, mod, fin,
                                 seq=seq, tm=tm, mod_row=mod_row, final=(l == DEPTH - 1))
            if name == 'p':
                new_k.append(k)
                new_v.append(v)
                new_dk.append(dk)
                new_dv.append(dv)
                new_st.append(st)

    y_prompt = xs['p'].reshape(bp, sp, D_MODEL)
    y_sample = xs['s'].reshape(bs, ss, D_MODEL)
    stack = lambda parts, shape: jnp.stack([p.reshape((bp,) + shape) for p in parts], axis=1)
    return (y_prompt, y_sample,
            stack(new_k, (sp, GQA_KV_HEADS, HEAD_DIM)),
            stack(new_v, (sp, GQA_KV_HEADS, HEAD_DIM)),
            stack(new_dk, (sp, DIFF_HEADS, 2, DIFF_QK_DIM)),
            stack(new_dv, (sp, DIFF_HEADS, HEAD_DIM)),
            jnp.stack(new_st, axis=1))
```

```python
import functools
import math

import numpy as np
import jax
import jax.numpy as jnp
from jax import lax
from jax.experimental import pallas as pl
from jax.experimental.pallas import tpu as pltpu

F32 = jnp.float32
BF16 = jnp.bfloat16

D_MODEL = 1024
DEPTH = 2
GRID_W = 64
HEAD_DIM = 64
LRU_WIDTH = 256
LRU_BLOCKS = 4
LRU_BLOCK_W = LRU_WIDTH // LRU_BLOCKS
LRU_C = 8.0
GQA_HEADS = 8
GQA_KV_HEADS = 2
DIFF_HEADS = 4
DIFF_QK_DIM = 32
D_FF = 2816
ROPE_THETA = 10000.0
EPS = 1e-6

LOG2_E = math.log2(math.e)
N_JOBS = GQA_HEADS + 2 * DIFF_HEADS
KEY_CHUNK = 256
LANES = 128
FFN_CHUNK = 256
N_FFN_CHUNKS = D_FF // FFN_CHUNK
COND_ROWS = 8
VMEM_LIMIT = 56 * 2 ** 20


def _params(*sem):
    return pltpu.CompilerParams(dimension_semantics=sem, vmem_limit_bytes=VMEM_LIMIT)


def _rms(x):
    return x * lax.rsqrt(jnp.mean(x * x, axis=-1, keepdims=True) + EPS)


def _ada_kernel(cond_ref, w_ref, b_ref, o_ref):
    c = cond_ref[...]
    s = c * jax.nn.sigmoid(c)
    o_ref[...] = jnp.dot(s.astype(BF16), w_ref[...].astype(BF16), preferred_element_type=F32) + b_ref[...]


def _ada(cond, ada_w, ada_b):
    tn = 1536
    width = 6 * D_MODEL
    return pl.pallas_call(
        _ada_kernel,
        grid=(DEPTH, width // tn),
        in_specs=[pl.BlockSpec((COND_ROWS, D_MODEL), lambda l, j: (0, 0)),
                  pl.BlockSpec((None, D_MODEL, tn), lambda l, j: (l, 0, j)),
                  pl.BlockSpec((None, 1, tn), lambda l, j: (l, 0, j))],
        out_specs=pl.BlockSpec((None, COND_ROWS, tn), lambda l, j: (l, 0, j)),
        out_shape=jax.ShapeDtypeStruct((DEPTH, COND_ROWS, width), F32),
        compiler_params=_params("arbitrary", "arbitrary"),
        name="ada",
    )(cond, ada_w, ada_b.reshape(DEPTH, 1, width))


def _group_mean_sq(x, ones_blk):
    sq = x * x
    hi = sq.astype(BF16)
    lo = (sq - hi.astype(F32)).astype(BF16)
    s = jnp.dot(hi, ones_blk, preferred_element_type=F32) + jnp.dot(lo, ones_blk, preferred_element_type=F32)
    return s * (1.0 / HEAD_DIM)


def _rope(x, cos, sin_neg, sin_pos, quarter):
    return x * cos + pltpu.roll(x, LANES - quarter, 1) * sin_neg + pltpu.roll(x, quarter, 1) * sin_pos


def _norm_proj_kernel(*refs, rope):
    if rope:
        (x_ref, mod_ref, n1_ref, w_ref, qg_ref, kg_ref, ones_ref, cq, snq, spq, cd, snd, spd,
         lru_ref, qx_ref, k_ref, v_ref, dk_ref, dv_ref) = refs
    else:
        (x_ref, mod_ref, n1_ref, w_ref, qg_ref, kg_ref, ones_ref,
         lru_ref, qx_ref, k_ref, v_ref, dk_ref, dv_ref) = refs
    mod = mod_ref[...]
    h = _rms(x_ref[...]) * (n1_ref[...] * (1.0 + mod[:, D_MODEL:2 * D_MODEL])) + mod[:, 0:D_MODEL]
    proj = jnp.dot(h.astype(BF16), w_ref[...], preferred_element_type=F32)
    lru_ref[...] = proj[:, 0:512]
    ones_blk = ones_ref[...]
    lane = lax.broadcasted_iota(jnp.int32, (1, LANES), 1)

    for c in range(4):
        xc = proj[:, 512 + LANES * c:512 + LANES * (c + 1)]
        xc = xc * lax.rsqrt(_group_mean_sq(xc, ones_blk) + EPS) * qg_ref[...]
        if rope:
            xc = _rope(xc, cq[...], snq[...], spq[...], HEAD_DIM // 4)
        xc = xc * (HEAD_DIM ** -0.5 * LOG2_E)
        xr = pltpu.roll(xc, HEAD_DIM, 1)
        for par in range(2):
            j = 2 * c + par
            want = j // (GQA_HEADS // GQA_KV_HEADS)
            src = xc if par == want else xr
            qx_ref[j] = jnp.where(lane // HEAD_DIM == want, src, 0.0).astype(BF16)

    kc = proj[:, 1024:1152]
    kc = kc * lax.rsqrt(_group_mean_sq(kc, ones_blk) + EPS) * kg_ref[...]
    if rope:
        kc = _rope(kc, cq[...], snq[...], spq[...], HEAD_DIM // 4)
    k_ref[...] = kc
    v_ref[...] = proj[:, 1152:1280]

    for c in range(2):
        xc = proj[:, 1280 + LANES * c:1280 + LANES * (c + 1)]
        if rope:
            xc = _rope(xc, cd[...], snd[...], spd[...], DIFF_QK_DIM // 4)
        xc = xc * (DIFF_QK_DIM ** -0.5 * LOG2_E)
        for g in range(4):
            qx_ref[GQA_HEADS + 4 * c + g] = jnp.where(lane // DIFF_QK_DIM == g, xc, 0.0).astype(BF16)
        kc = proj[:, 1536 + LANES * c:1536 + LANES * (c + 1)]
        if rope:
            kc = _rope(kc, cd[...], snd[...], spd[...], DIFF_QK_DIM // 4)
        dk_ref[:, LANES * c:LANES * (c + 1)] = kc
    dv_ref[...] = proj[:, 1792:2048]


def _norm_proj(x, mod, n1, w_in, qg, kg, ones_blk, tables, *, seq, tm, mod_row):
    n = x.shape[0]
    rope = tables is not None
    nts = seq // tm
    row = lambda i: (i, 0)
    const = lambda i: (0, 0)
    in_specs = [pl.BlockSpec((tm, D_MODEL), row),
                pl.BlockSpec((None, 1, 6 * D_MODEL), lambda i: (mod_row(i // nts), 0, 0)),
                pl.BlockSpec((1, D_MODEL), const),
                pl.BlockSpec((D_MODEL, 2048), const),
                pl.BlockSpec((1, LANES), const),
                pl.BlockSpec((1, LANES), const),
                pl.BlockSpec((LANES, LANES), const)]
    args = [x, mod, n1, w_in, qg, kg, ones_blk]
    if rope:
        in_specs += [pl.BlockSpec((tm, LANES), lambda i: (i % nts, 0))] * 6
        args += list(tables)
    slab = lambda i: (0, i, 0)
    out_specs = [pl.BlockSpec((tm, 512), row),
                 pl.BlockSpec((N_JOBS, tm, LANES), slab),
                 pl.BlockSpec((tm, 128), row),
                 pl.BlockSpec((tm, 128), row),
                 pl.BlockSpec((tm, 256), row),
                 pl.BlockSpec((tm, 256), row)]
    out_shape = [jax.ShapeDtypeStruct((n, 512), F32),
                 jax.ShapeDtypeStruct((N_JOBS, n, LANES), BF16),
                 jax.ShapeDtypeStruct((n, 128), F32),
                 jax.ShapeDtypeStruct((n, 128), F32),
                 jax.ShapeDtypeStruct((n, 256), F32),
                 jax.ShapeDtypeStruct((n, 256), F32)]
    return pl.pallas_call(
        functools.partial(_norm_proj_kernel, rope=rope),
        grid=(n // tm,), in_specs=in_specs, out_specs=out_specs, out_shape=out_shape,
        compiler_params=_params("arbitrary"), name="norm_proj",
    )(*args)


def _chunk_scan(a, u, row, reverse):
    for d in (1, 2, 4):
        shift = 8 - d if reverse else d
        a_s = pltpu.roll(a, shift, 0)
        u_s = pltpu.roll(u, shift, 0)
        m = (row < 8 - d) if reverse else (row >= d)
        u = jnp.where(m, a * u_s + u, u)
        a = jnp.where(m, a * a_s, a)
    return a, u


def _lru_kernel(x_ref, xp_ref, xn_ref, cw_ref, cb_ref, gw_ref, gb_ref, lam_ref, h0_ref, out_ref, st_ref,
                af, uf, ab, ub, gg, *, seq, tr):
    j = pl.program_id(1)
    nt = seq // tr
    xg = x_ref[...]
    x = xg[:, 0:LRU_WIDTH]
    prev = jnp.where(j > 0, xp_ref[...], 0.0)
    nxt = jnp.where(j < nt - 1, xn_ref[...], 0.0)
    xe = jnp.concatenate([prev, x, nxt], axis=0)
    ne = tr + 16
    cw = cw_ref[...]
    xc = (cw[0:1] * pltpu.roll(xe, 2, 0)[8:8 + tr] + cw[1:2] * pltpu.roll(xe, 1, 0)[8:8 + tr]
          + cw[2:3] * x + cw[3:4] * pltpu.roll(xe, ne - 1, 0)[8:8 + tr] + cb_ref[...])
    sg = jax.nn.sigmoid(jnp.dot(xc.astype(BF16), gw_ref[...], preferred_element_type=F32) + gb_ref[...])
    z = -lam_ref[...]
    neg_c_softplus = -LRU_C * (jnp.maximum(z, 0.0) + jnp.log1p(jnp.exp(-jnp.abs(z))))
    r0 = pl.multiple_of(j * tr, tr)
    for d, (a_s, u_s) in enumerate(((af, uf), (ab, ub))):
        r = sg[:, 512 * d:512 * d + LRU_WIDTH]
        i = sg[:, 512 * d + LRU_WIDTH:512 * (d + 1)]
        log_a = neg_c_softplus[d:d + 1] * r
        a = jnp.exp(log_a)
        a_s[pl.ds(r0, tr), :] = a
        u_s[pl.ds(r0, tr), :] = jnp.sqrt(-jnp.tanh(log_a) * (a * a + 1.0)) * (i * xc)
    gg[pl.ds(r0, tr), :] = jax.nn.gelu(xg[:, LRU_WIDTH:2 * LRU_WIDTH])

    @pl.when(j == nt - 1)
    def _():
        row = lax.broadcasted_iota(jnp.int32, (8, LRU_WIDTH), 0)
        nchunk = seq // 8

        def body(c, carry):
            hf, hb = carry
            rf = pl.multiple_of(c * 8, 8)
            a, u = _chunk_scan(af[pl.ds(rf, 8), :], uf[pl.ds(rf, 8), :], row, False)
            hs = u + a * hf
            uf[pl.ds(rf, 8), :] = hs
            rb = pl.multiple_of((nchunk - 1 - c) * 8, 8)
            a2, u2 = _chunk_scan(ab[pl.ds(rb, 8), :], ub[pl.ds(rb, 8), :], row, True)
            hs2 = u2 + a2 * hb
            ub[pl.ds(rb, 8), :] = hs2
            return hs[7:8, :], hs2[0:1, :]

        h0 = h0_ref[...]
        hf, hb = lax.fori_loop(0, nchunk, body, (h0[0:1], h0[1:2]))
        st_ref[...] = jnp.concatenate([hf, hb], axis=0)

        def obody(t, carry):
            r = pl.multiple_of(t * tr, tr)
            out_ref[pl.ds(r, tr), :] = (gg[pl.ds(r, tr), :] * (uf[pl.ds(r, tr), :] + ub[pl.ds(r, tr), :])).astype(BF16)
            return carry

        lax.fori_loop(0, nt, obody, 0)


def _lru(lru, cw, cb, gw, gb, lam, h0, *, seq, tr):
    n = lru.shape[0]
    b = n // seq
    nt = seq // tr
    nb8 = n // 8
    const = lambda bi, j: (0, 0)
    in_specs = [pl.BlockSpec((tr, 512), lambda bi, j: (bi * nt + j, 0)),
                pl.BlockSpec((8, LRU_WIDTH), lambda bi, j: (jnp.maximum((bi * nt + j) * (tr // 8) - 1, 0), 0)),
                pl.BlockSpec((8, LRU_WIDTH), lambda bi, j: (jnp.minimum((bi * nt + j + 1) * (tr // 8), nb8 - 1), 0)),
                pl.BlockSpec((4, LRU_WIDTH), const),
                pl.BlockSpec((1, LRU_WIDTH), const),
                pl.BlockSpec((LRU_WIDTH, 4 * LRU_WIDTH), const),
                pl.BlockSpec((1, 4 * LRU_WIDTH), const),
                pl.BlockSpec((2, LRU_WIDTH), const),
                pl.BlockSpec((None, 2, LRU_WIDTH), lambda bi, j: (bi, 0, 0))]
    out_specs = [pl.BlockSpec((seq, LRU_WIDTH), lambda bi, j: (bi, 0)),
                 pl.BlockSpec((None, 2, LRU_WIDTH), lambda bi, j: (bi, 0, 0))]
    out_shape = [jax.ShapeDtypeStruct((n, LRU_WIDTH), BF16),
                 jax.ShapeDtypeStruct((b, 2, LRU_WIDTH), F32)]
    return pl.pallas_call(
        functools.partial(_lru_kernel, seq=seq, tr=tr),
        grid=(b, nt), in_specs=in_specs, out_specs=out_specs, out_shape=out_shape,
        scratch_shapes=[pltpu.VMEM((seq, LRU_WIDTH), F32)] * 5,
        compiler_params=_params("arbitrary", "arbitrary"), name="lru",
    )(lru, lru, lru, cw, cb, gw, gb, lam, h0)


_NT = (((1,), (1,)), ((), ()))


def _attn_kernel(*refs, n_ctx, seq, tq, lam_init):
    if n_ctx:
        (q_ref, k_ref, v_ref, dk_ref, dv_ref, ck_ref, cv_ref, cdk_ref, cdv_ref, dl_ref, dg_ref,
         gqa_ref, dout_ref, ks, vts, s_buf0, s_buf1, e_buf0, e_buf1, o_buf) = refs
    else:
        (q_ref, k_ref, v_ref, dk_ref, dv_ref, dl_ref, dg_ref,
         gqa_ref, dout_ref, ks, vts, s_buf0, s_buf1, e_buf0, e_buf1, o_buf) = refs
    s_bufs = (s_buf0, s_buf1)
    e_bufs = (e_buf0, e_buf1)
    t_all = n_ctx + seq
    n_chunks = t_all // KEY_CHUNK

    @pl.when(pl.program_id(1) == 0)
    def _():
        def put(slab, new, ctx):
            if n_ctx:
                ks[slab, 0:n_ctx, :] = ctx[0].astype(BF16)
                vts[slab, :, 0:n_ctx] = ctx[1].T.astype(BF16)
            ks[slab, n_ctx:t_all, :] = new[0].astype(BF16)
            vts[slab, :, n_ctx:t_all] = new[1].T.astype(BF16)

        put(0, (k_ref[...], v_ref[...]), (ck_ref[...], cv_ref[...]) if n_ctx else None)
        for c in range(2):
            cols = slice(LANES * c, LANES * (c + 1))
            put(1 + c, (dk_ref[:, cols], dv_ref[:, cols]), (cdk_ref[:, cols], cdv_ref[:, cols]) if n_ctx else None)

    def slab_of(g):
        return jnp.where(g < GQA_HEADS, 0, 1 + (g - GQA_HEADS) // 4)

    def fold8(x, op):
        acc = x[0:8]
        for r in range(1, KEY_CHUNK // 8):
            acc = op(acc, x[8 * r:8 * (r + 1)])
        return acc

    def run(t, par, m_prev, l_prev, do_a=True, do_b=True, do_c=True):
        ga, gb, gc = t, t - 1, t - 2
        s_w, s_r = s_bufs[par], s_bufs[1 - par]
        e_w, e_r = e_bufs[1 - par], e_bufs[par]
        m_acc = l_acc = None
        if do_a:
            qa = q_ref[ga]
            ka = slab_of(ga)
            m_acc = jnp.full((8, tq), -jnp.inf, F32)
        if do_b:
            m_row = jnp.max(m_prev, axis=0, keepdims=True)
            l_acc = jnp.zeros((8, tq), F32)
        if do_c:
            vc = slab_of(gc)
            o_acc = jnp.zeros((LANES, tq), F32)
        for c in range(n_chunks):
            rows = slice(KEY_CHUNK * c, KEY_CHUNK * (c + 1))
            if do_a:
                s = lax.dot_general(ks[ka, rows, :], qa, _NT, preferred_element_type=F32)
                s_w[rows, :] = s
                m_acc = jnp.maximum(m_acc, fold8(s, jnp.maximum))
            if do_b:
                e = jnp.exp2(s_r[rows, :] - m_row)
                e_w[rows, :] = e.astype(BF16)
                l_acc = l_acc + fold8(e, jnp.add)
            if do_c:
                o_acc = o_acc + jnp.dot(vts[vc, :, rows], e_r[rows, :], preferred_element_type=F32)
        if do_c:
            o_buf[gc] = o_acc * (1.0 / jnp.sum(l_prev, axis=0, keepdims=True))
        return m_acc, l_acc

    m, _ = run(0, 0, None, None, do_b=False, do_c=False)
    m, l = run(1, 1, m, None, do_c=False)

    def steady(i, carry):
        m, l = run(2 * i, 0, *carry)
        return run(2 * i + 1, 1, m, l)

    m, l = lax.fori_loop(1, N_JOBS // 2, steady, (m, l))
    _, l = run(N_JOBS, 0, m, l, do_a=False)
    run(N_JOBS + 1, 1, None, l, do_a=False, do_b=False)

    lane = lax.broadcasted_iota(jnp.int32, (1, LANES), 1)
    low_half = lane < HEAD_DIM

    for c in range(4):
        h = c // 2
        rows = slice(HEAD_DIM * h, HEAD_DIM * (h + 1))
        ot = jnp.concatenate([o_buf[2 * c, rows, :], o_buf[2 * c + 1, rows, :]], axis=0)
        gqa_ref[:, LANES * c:LANES * (c + 1)] = ot.T.astype(BF16)

    dl = dl_ref[...]
    lam = (jnp.exp(jnp.sum(dl[0:1] * dl[1:2], axis=-1, keepdims=True))
           - jnp.exp(jnp.sum(dl[2:3] * dl[3:4], axis=-1, keepdims=True)) + lam_init)

    for c in range(2):
        parts = []
        for par in range(2):
            g1 = GQA_HEADS + 2 * (2 * c + par)
            rows = slice(HEAD_DIM * par, HEAD_DIM * (par + 1))
            parts.append(o_buf[g1, rows, :] - lam * o_buf[g1 + 1, rows, :])
        o = jnp.concatenate(parts, axis=0).T
        sq = o * o
        ms = jnp.where(low_half,
                       jnp.sum(jnp.where(low_half, sq, 0.0), axis=-1, keepdims=True),
                       jnp.sum(jnp.where(low_half, 0.0, sq), axis=-1, keepdims=True)) * (1.0 / HEAD_DIM)
        dout_ref[:, LANES * c:LANES * (c + 1)] = (
            o * lax.rsqrt(ms + EPS) * dg_ref[...] * (1.0 - lam_init)).astype(BF16)


def _attention(qx, k, v, dk, dv, caches, dl, dg, *, layer, seq, tq, lam_init):
    n = k.shape[0]
    b = n // seq
    nq = seq // tq
    n_ctx = 0 if caches is None else caches[0].shape[2]
    t_all = n_ctx + seq
    slab = lambda bi, i: (0, bi * nq + i, 0)
    per_b = lambda bi, i: (bi, 0)
    const = lambda bi, i: (0, 0)
    in_specs = [pl.BlockSpec((N_JOBS, tq, LANES), slab),
                pl.BlockSpec((seq, 128), per_b),
                pl.BlockSpec((seq, 128), per_b),
                pl.BlockSpec((seq, 256), per_b),
                pl.BlockSpec((seq, 256), per_b)]
    args = [qx, k, v, dk, dv]
    if n_ctx:
        cache_idx = lambda bi, i: (bi, layer, 0, 0)
        in_specs += [pl.BlockSpec((None, None, n_ctx, 128), cache_idx),
                     pl.BlockSpec((None, None, n_ctx, 128), cache_idx),
                     pl.BlockSpec((None, None, n_ctx, 256), cache_idx),
                     pl.BlockSpec((None, None, n_ctx, 256), cache_idx)]
        args += list(caches)
    in_specs += [pl.BlockSpec((4, DIFF_QK_DIM), const), pl.BlockSpec((1, LANES), const)]
    args += [dl, dg]
    row = lambda bi, i: (bi * nq + i, 0)
    return pl.pallas_call(
        functools.partial(_attn_kernel, n_ctx=n_ctx, seq=seq, tq=tq, lam_init=lam_init),
        grid=(b, nq), in_specs=in_specs,
        out_specs=[pl.BlockSpec((tq, 512), row), pl.BlockSpec((tq, 256), row)],
        out_shape=[jax.ShapeDtypeStruct((n, 512), BF16), jax.ShapeDtypeStruct((n, 256), BF16)],
        scratch_shapes=[pltpu.VMEM((3, t_all, LANES), BF16), pltpu.VMEM((3, LANES, t_all), BF16),
                        pltpu.VMEM((t_all, tq), F32), pltpu.VMEM((t_all, tq), F32),
                        pltpu.VMEM((t_all, tq), BF16), pltpu.VMEM((t_all, tq), BF16),
                        pltpu.VMEM((N_JOBS, LANES, tq), F32)],
        compiler_params=_params("arbitrary", "arbitrary"), name="attention",
    )(*args)


def _out_proj_kernel(lru_ref, gqa_ref, d_ref, w_ref, x_ref, mod_ref, n2_ref, x1_ref, h2_ref):
    m = (jnp.dot(lru_ref[...], w_ref[0:256, :], preferred_element_type=F32)
         + jnp.dot(gqa_ref[...], w_ref[256:768, :], preferred_element_type=F32)
         + jnp.dot(d_ref[...], w_ref[768:1024, :], preferred_element_type=F32))
    mod = mod_ref[...]
    x1 = x_ref[...] + mod[:, 2 * D_MODEL:3 * D_MODEL] * m
    x1_ref[...] = x1
    h2_ref[...] = (_rms(x1) * (n2_ref[...] * (1.0 + mod[:, 4 * D_MODEL:5 * D_MODEL]))
                   + mod[:, 3 * D_MODEL:4 * D_MODEL]).astype(BF16)


def _out_proj(lru_o, gqa_o, d_o, w_out, x, mod, n2, *, seq, tm, mod_row):
    n = x.shape[0]
    nts = seq // tm
    row = lambda i: (i, 0)
    const = lambda i: (0, 0)
    return pl.pallas_call(
        _out_proj_kernel,
        grid=(n // tm,),
        in_specs=[pl.BlockSpec((tm, 256), row), pl.BlockSpec((tm, 512), row), pl.BlockSpec((tm, 256), row),
                  pl.BlockSpec((D_MODEL, D_MODEL), const),
                  pl.BlockSpec((tm, D_MODEL), row),
                  pl.BlockSpec((None, 1, 6 * D_MODEL), lambda i: (mod_row(i // nts), 0, 0)),
                  pl.BlockSpec((1, D_MODEL), const)],
        out_specs=[pl.BlockSpec((tm, D_MODEL), row), pl.BlockSpec((tm, D_MODEL), row)],
        out_shape=[jax.ShapeDtypeStruct((n, D_MODEL), F32), jax.ShapeDtypeStruct((n, D_MODEL), BF16)],
        compiler_params=_params("arbitrary"), name="out_proj",
    )(lru_o, gqa_o, d_o, w_out, x, mod, n2)


HALO = 16


def _ffn_kernel(h_ref, hp_ref, hn_ref, x1_ref, wa_ref, wg_ref, cwa_ref, cba_ref, cwg_ref, cbg_ref, wd_ref,
                mod_ref, fin_ref, o_ref, he, acc, *, seq, tm, final):
    i = pl.program_id(0)
    nts = seq // tm
    pos = i % nts

    @pl.when(pos == 0)
    def _():
        he[0:HALO, :] = jnp.zeros((HALO, D_MODEL), BF16)

    @pl.when(pos != 0)
    def _():
        he[0:HALO, :] = hp_ref[...]

    @pl.when(pos == nts - 1)
    def _():
        he[HALO + tm:2 * HALO + tm, :] = jnp.zeros((HALO, D_MODEL), BF16)

    @pl.when(pos != nts - 1)
    def _():
        he[HALO + tm:2 * HALO + tm, :] = hn_ref[...]

    he[HALO:HALO + tm, :] = h_ref[...]
    acc[...] = jnp.zeros((tm, D_MODEL), F32)
    ne = tm + 2 * HALO

    def conv(u, cw, cb):
        return (cw[0:1] * pltpu.roll(u, 1, 0)[HALO:HALO + tm] + cw[1:2] * u[HALO:HALO + tm]
                + cw[2:3] * pltpu.roll(u, ne - 1, 0)[HALO:HALO + tm] + cb)

    def body(c, carry):
        lhs = he[...]
        a = conv(jnp.dot(lhs, wa_ref[c], preferred_element_type=F32), cwa_ref[c], cba_ref[c])
        g = conv(jnp.dot(lhs, wg_ref[c], preferred_element_type=F32), cwg_ref[c], cbg_ref[c])
        act = (g * jax.nn.sigmoid(g) * a).astype(BF16)
        acc[...] += jnp.dot(act, wd_ref[c], preferred_element_type=F32)
        return carry

    lax.fori_loop(0, N_FFN_CHUNKS, body, 0)
    mod = mod_ref[...]
    x2 = x1_ref[...] + mod[:, 5 * D_MODEL:6 * D_MODEL] * acc[...]
    if final:
        x2 = _rms(x2) * fin_ref[...]
    o_ref[...] = x2


def _conv_ffn(h2, x1, wa, wg, cwa, cba, cwg, cbg, wd, mod, fin_g, *, seq, tm, mod_row, final):
    n = x1.shape[0]
    nts = seq // tm
    nbh = n // HALO
    row = lambda i: (i, 0)
    c2 = lambda i: (0, 0)
    c3 = lambda i: (0, 0, 0)
    in_specs = [pl.BlockSpec((tm, D_MODEL), row),
                pl.BlockSpec((HALO, D_MODEL), lambda i: (jnp.maximum(i * (tm // HALO) - 1, 0), 0)),
                pl.BlockSpec((HALO, D_MODEL), lambda i: (jnp.minimum((i + 1) * (tm // HALO), nbh - 1), 0)),
                pl.BlockSpec((tm, D_MODEL), row),
                pl.BlockSpec((N_FFN_CHUNKS, D_MODEL, FFN_CHUNK), c3),
                pl.BlockSpec((N_FFN_CHUNKS, D_MODEL, FFN_CHUNK), c3),
                pl.BlockSpec((N_FFN_CHUNKS, 3, FFN_CHUNK), c3),
                pl.BlockSpec((N_FFN_CHUNKS, 1, FFN_CHUNK), c3),
                pl.BlockSpec((N_FFN_CHUNKS, 3, FFN_CHUNK), c3),
                pl.BlockSpec((N_FFN_CHUNKS, 1, FFN_CHUNK), c3),
                pl.BlockSpec((N_FFN_CHUNKS, FFN_CHUNK, D_MODEL), c3),
                pl.BlockSpec((None, 1, 6 * D_MODEL), lambda i: (mod_row(i // nts), 0, 0)),
                pl.BlockSpec((1, D_MODEL), c2)]
    return pl.pallas_call(
        functools.partial(_ffn_kernel, seq=seq, tm=tm, final=final),
        grid=(n // tm,), in_specs=in_specs,
        out_specs=pl.BlockSpec((tm, D_MODEL), row),
        out_shape=jax.ShapeDtypeStruct((n, D_MODEL), F32),
        scratch_shapes=[pltpu.VMEM((tm + 2 * HALO, D_MODEL), BF16), pltpu.VMEM((tm, D_MODEL), F32)],
        compiler_params=_params("arbitrary"), name="conv_ffn",
    )(h2, h2, h2, x1, wa, wg, cwa, cba, cwg, cbg, wd, mod, fin_g)


def _rope_tables(seq, dim):
    rows = seq // GRID_W
    t_row = np.repeat(np.arange(rows, dtype=np.float64), GRID_W)
    t_col = np.tile(np.arange(GRID_W, dtype=np.float64), rows)
    axis_dim = dim // 2
    inv = ROPE_THETA ** (-np.arange(0, axis_dim, 2, dtype=np.float64) / axis_dim)
    ar = t_row[:, None] * inv
    ac = t_col[:, None] * inv
    ang = np.concatenate([ar, ar, ac, ac], axis=-1)
    reps = LANES // dim
    cos = np.tile(np.cos(ang), (1, reps))
    sin = np.tile(np.sin(ang), (1, reps))
    first = (np.arange(LANES) % (dim // 2)) < (dim // 4)
    sin_neg = np.where(first, -sin, 0.0)
    sin_pos = np.where(first, 0.0, sin)
    return tuple(jnp.asarray(t, F32) for t in (cos, sin_neg, sin_pos))


def _gate_weights(w, b):
    eye = jnp.eye(LRU_BLOCKS, dtype=w.dtype)
    cols, biases = [], []
    for d in range(2):
        for part in range(2):
            blk = w[d, :, :, part * LRU_BLOCK_W:(part + 1) * LRU_BLOCK_W]
            cols.append(jnp.einsum('nde,nm->ndme', blk, eye).reshape(LRU_WIDTH, LRU_WIDTH))
            biases.append(b[d, :, part * LRU_BLOCK_W:(part + 1) * LRU_BLOCK_W].reshape(LRU_WIDTH))
    return jnp.concatenate(cols, axis=1).astype(BF16), jnp.concatenate(biases)[None, :]


def _chunked_cols(w):
    return jnp.transpose(w.reshape(w.shape[0], N_FFN_CHUNKS, FFN_CHUNK), (1, 0, 2))


def kernel(x_prompt, x_sample, cache_gqa_k, cache_gqa_v, cache_diff_k, cache_diff_v, state_lru, c, c_ctx,
           norm1_g, norm2_g, final_norm_g, ada_w, ada_b, w_in, w_out, lru_conv_w, lru_conv_b, lru_gate_w,
           lru_gate_b, lru_lambda, gqa_q_norm_g, gqa_k_norm_g, diff_lambda, diff_norm_g, ffn_w_up, ffn_conv_w,
           ffn_conv_b, ffn_w_down):
    bp, sp, _ = x_prompt.shape
    bs, ss, _ = x_sample.shape
    n_ctx = cache_gqa_k.shape[2]

    cond = jnp.concatenate([c_ctx[None, :], c, jnp.zeros((COND_ROWS - 1 - bs, D_MODEL), F32)], axis=0)
    mod_all = _ada(cond, ada_w, ada_b)

    caches = (cache_gqa_k.reshape(bs, DEPTH, n_ctx, 128), cache_gqa_v.reshape(bs, DEPTH, n_ctx, 128),
              cache_diff_k.reshape(bs, DEPTH, n_ctx, 256), cache_diff_v.reshape(bs, DEPTH, n_ctx, 256))
    tables = _rope_tables(ss, HEAD_DIM) + _rope_tables(ss, DIFF_QK_DIM)
    ones_blk = jnp.asarray(np.kron(np.eye(2), np.ones((HEAD_DIM, HEAD_DIM))), BF16)
    zero_state = jnp.zeros((bp, 2, LRU_WIDTH), F32)

    groups = {
        'p': dict(seq=sp, tm=256, tq=256, mod_row=lambda b: 0, tables=None, caches=None),
        's': dict(seq=ss, tm=512, tq=256, mod_row=lambda b: 1 + b, tables=tables, caches=caches),
    }
    xs = {'p': x_prompt.reshape(bp * sp, D_MODEL), 's': x_sample.reshape(bs * ss, D_MODEL)}
    new_k, new_v, new_dk, new_dv, new_st = [], [], [], [], []

    for l in range(DEPTH):
        lam_init = 0.8 - 0.6 * math.exp(-0.3 * l)
        mod = mod_all[l].reshape(COND_ROWS, 1, 6 * D_MODEL)
        w_in_l = w_in[l].astype(BF16)
        w_out_l = w_out[l].astype(BF16)
        gw, gb = _gate_weights(lru_gate_w[l], lru_gate_b[l])
        qg = jnp.tile(gqa_q_norm_g[l], 2)[None, :]
        kg = jnp.tile(gqa_k_norm_g[l], 2)[None, :]
        dg = jnp.tile(diff_norm_g[l], 2)[None, :]
        up = ffn_w_up[l].astype(BF16)
        wa, wg = _chunked_cols(up[:, :D_FF]), _chunked_cols(up[:, D_FF:])
        cwa, cwg = _chunked_cols(ffn_conv_w[l][:, :D_FF]), _chunked_cols(ffn_conv_w[l][:, D_FF:])
        cba, cbg = _chunked_cols(ffn_conv_b[l][None, :D_FF]), _chunked_cols(ffn_conv_b[l][None, D_FF:])
        wd = ffn_w_down[l].astype(BF16).reshape(N_FFN_CHUNKS, FFN_CHUNK, D_MODEL)
        n1 = norm1_g[l][None, :]
        n2 = norm2_g[l][None, :]
        fin = final_norm_g[None, :]

        for name in ('p', 's'):
            g = groups[name]
            seq, tm, mod_row = g['seq'], g['tm'], g['mod_row']
            x = xs[name]
            lru, qx, k, v, dk, dv = _norm_proj(x, mod, n1, w_in_l, qg, kg, ones_blk, g['tables'],
                                               seq=seq, tm=tm, mod_row=mod_row)
            h0 = zero_state if name == 'p' else state_lru[:, l]
            lru_o, st = _lru(lru, lru_conv_w[l], lru_conv_b[l][None, :], gw, gb, lru_lambda[l], h0, seq=seq, tr=256)
            gqa_o, d_o = _attention(qx, k, v, dk, dv, g['caches'], diff_lambda[l], dg,
                                    layer=l, seq=seq, tq=g['tq'], lam_init=lam_init)
            x1, h2 = _out_proj(lru_o, gqa_o, d_o, w_out_l, x, mod, n2, seq=seq, tm=tm, mod_row=mod_row)
            xs[name] = _conv_ffn(h2, x1, wa, wg, cwa, cba, cwg, cbg, wd, mod, fin,
                                 seq=seq, tm=tm, mod_row=mod_row, final=(l == DEPTH - 1))
            if name == 'p':
                new_k.append(k)
                new_v.append(v)
                new_dk.append(dk)
                new_dv.append(dv)
                new_st.append(st)

    y_prompt = xs['p'].reshape(bp, sp, D_MODEL)
    y_sample = xs['s'].reshape(bs, ss, D_MODEL)
    stack = lambda parts, shape: jnp.stack([p.reshape((bp,) + shape) for p in parts], axis=1)
    return (y_prompt, y_sample,
            stack(new_k, (sp, GQA_KV_HEADS, HEAD_DIM)),
            stack(new_v, (sp, GQA_KV_HEADS, HEAD_DIM)),
            stack(new_dk, (sp, DIFF_HEADS, 2, DIFF_QK_DIM)),
            stack(new_dv, (sp, DIFF_HEADS, HEAD_DIM)),
            jnp.stack(new_st, axis=1))
```

```python
import functools
import math

import numpy as np
import jax
import jax.numpy as jnp
from jax import lax
from jax.experimental import pallas as pl
from jax.experimental.pallas import tpu as pltpu

F32 = jnp.float32
BF16 = jnp.bfloat16

D_MODEL = 1024
DEPTH = 2
GRID_W = 64
HEAD_DIM = 64
LRU_WIDTH = 256
LRU_BLOCKS = 4
LRU_BLOCK_W = LRU_WIDTH // LRU_BLOCKS
LRU_C = 8.0
GQA_HEADS = 8
GQA_KV_HEADS = 2
DIFF_HEADS = 4
DIFF_QK_DIM = 32
D_FF = 2816
ROPE_THETA = 10000.0
EPS = 1e-6

LOG2_E = math.log2(math.e)
N_JOBS = GQA_HEADS + 2 * DIFF_HEADS
KEY_CHUNK = 256
ATTN_WIDTH = 512
LANES = 128
FFN_CHUNK = 256
N_FFN_CHUNKS = D_FF // FFN_CHUNK
COND_ROWS = 8
VMEM_LIMIT = 56 * 2 ** 20


def _params(*sem):
    return pltpu.CompilerParams(dimension_semantics=sem, vmem_limit_bytes=VMEM_LIMIT)


def _rms(x):
    return x * lax.rsqrt(jnp.mean(x * x, axis=-1, keepdims=True) + EPS)


def _ada_kernel(cond_ref, w_ref, b_ref, o_ref):
    c = cond_ref[...]
    s = c * jax.nn.sigmoid(c)
    o_ref[...] = jnp.dot(s.astype(BF16), w_ref[...].astype(BF16), preferred_element_type=F32) + b_ref[...]


def _ada(cond, ada_w, ada_b):
    tn = 1536
    width = 6 * D_MODEL
    return pl.pallas_call(
        _ada_kernel,
        grid=(DEPTH, width // tn),
        in_specs=[pl.BlockSpec((COND_ROWS, D_MODEL), lambda l, j: (0, 0)),
                  pl.BlockSpec((None, D_MODEL, tn), lambda l, j: (l, 0, j)),
                  pl.BlockSpec((None, 1, tn), lambda l, j: (l, 0, j))],
        out_specs=pl.BlockSpec((None, COND_ROWS, tn), lambda l, j: (l, 0, j)),
        out_shape=jax.ShapeDtypeStruct((DEPTH, COND_ROWS, width), F32),
        compiler_params=_params("arbitrary", "arbitrary"),
        name="ada",
    )(cond, ada_w, ada_b.reshape(DEPTH, 1, width))


def _group_mean_sq(x, ones_blk):
    sq = x * x
    hi = sq.astype(BF16)
    lo = (sq - hi.astype(F32)).astype(BF16)
    s = jnp.dot(hi, ones_blk, preferred_element_type=F32) + jnp.dot(lo, ones_blk, preferred_element_type=F32)
    return s * (1.0 / HEAD_DIM)


def _rope(x, cos, sin_neg, sin_pos, quarter):
    return x * cos + pltpu.roll(x, LANES - quarter, 1) * sin_neg + pltpu.roll(x, quarter, 1) * sin_pos


def _norm_proj_kernel(*refs, rope):
    if rope:
        (x_ref, mod_ref, n1_ref, w_ref, qg_ref, kg_ref, ones_ref, cq, snq, spq, cd, snd, spd,
         lru_ref, qx_ref, k_ref, v_ref, dk_ref, dv_ref) = refs
    else:
        (x_ref, mod_ref, n1_ref, w_ref, qg_ref, kg_ref, ones_ref,
         lru_ref, qx_ref, k_ref, v_ref, dk_ref, dv_ref) = refs
    mod = mod_ref[...]
    h = _rms(x_ref[...]) * (n1_ref[...] * (1.0 + mod[:, D_MODEL:2 * D_MODEL])) + mod[:, 0:D_MODEL]
    proj = jnp.dot(h.astype(BF16), w_ref[...], preferred_element_type=F32)
    lru_ref[...] = proj[:, 0:512]
    ones_blk = ones_ref[...]
    lane = lax.broadcasted_iota(jnp.int32, (1, LANES), 1)

    for c in range(4):
        xc = proj[:, 512 + LANES * c:512 + LANES * (c + 1)]
        xc = xc * lax.rsqrt(_group_mean_sq(xc, ones_blk) + EPS) * qg_ref[...]
        if rope:
            xc = _rope(xc, cq[...], snq[...], spq[...], HEAD_DIM // 4)
        xc = xc * (HEAD_DIM ** -0.5 * LOG2_E)
        xr = pltpu.roll(xc, HEAD_DIM, 1)
        for par in range(2):
            j = 2 * c + par
            want = j // (GQA_HEADS // GQA_KV_HEADS)
            src = xc if par == want else xr
            qx_ref[j] = jnp.where(lane // HEAD_DIM == want, src, 0.0).astype(BF16)

    kc = proj[:, 1024:1152]
    kc = kc * lax.rsqrt(_group_mean_sq(kc, ones_blk) + EPS) * kg_ref[...]
    if rope:
        kc = _rope(kc, cq[...], snq[...], spq[...], HEAD_DIM // 4)
    k_ref[...] = kc
    v_ref[...] = proj[:, 1152:1280]

    for c in range(2):
        xc = proj[:, 1280 + LANES * c:1280 + LANES * (c + 1)]
        if rope:
            xc = _rope(xc, cd[...], snd[...], spd[...], DIFF_QK_DIM // 4)
        xc = xc * (DIFF_QK_DIM ** -0.5 * LOG2_E)
        for g in range(4):
            qx_ref[GQA_HEADS + 4 * c + g] = jnp.where(lane // DIFF_QK_DIM == g, xc, 0.0).astype(BF16)
        kc = proj[:, 1536 + LANES * c:1536 + LANES * (c + 1)]
        if rope:
            kc = _rope(kc, cd[...], snd[...], spd[...], DIFF_QK_DIM // 4)
        dk_ref[:, LANES * c:LANES * (c + 1)] = kc
    dv_ref[...] = proj[:, 1792:2048]


def _norm_proj(x, mod, n1, w_in, qg, kg, ones_blk, tables, *, layer, seq, tm, mod_row):
    n = x.shape[0]
    rope = tables is not None
    nts = seq // tm
    row = lambda i: (i, 0)
    const = lambda i: (0, 0)
    in_specs = [pl.BlockSpec((tm, D_MODEL), row),
                pl.BlockSpec((None, 1, 6 * D_MODEL), lambda i: (mod_row(i // nts), 0, 0)),
                pl.BlockSpec((1, D_MODEL), const),
                pl.BlockSpec((None, D_MODEL, 2048), lambda i: (layer, 0, 0)),
                pl.BlockSpec((1, LANES), const),
                pl.BlockSpec((1, LANES), const),
                pl.BlockSpec((LANES, LANES), const)]
    args = [x, mod, n1, w_in, qg, kg, ones_blk]
    if rope:
        in_specs += [pl.BlockSpec((tm, LANES), lambda i: (i % nts, 0))] * 6
        args += list(tables)
    slab = lambda i: (0, i, 0)
    out_specs = [pl.BlockSpec((tm, 512), row),
                 pl.BlockSpec((N_JOBS, tm, LANES), slab),
                 pl.BlockSpec((tm, 128), row),
                 pl.BlockSpec((tm, 128), row),
                 pl.BlockSpec((tm, 256), row),
                 pl.BlockSpec((tm, 256), row)]
    out_shape = [jax.ShapeDtypeStruct((n, 512), F32),
                 jax.ShapeDtypeStruct((N_JOBS, n, LANES), BF16),
                 jax.ShapeDtypeStruct((n, 128), F32),
                 jax.ShapeDtypeStruct((n, 128), F32),
                 jax.ShapeDtypeStruct((n, 256), F32),
                 jax.ShapeDtypeStruct((n, 256), F32)]
    return pl.pallas_call(
        functools.partial(_norm_proj_kernel, rope=rope),
        grid=(n // tm,), in_specs=in_specs, out_specs=out_specs, out_shape=out_shape,
        compiler_params=_params("arbitrary"), name="norm_proj",
    )(*args)


def _chunk_scan(a, u, row, reverse):
    for d in (1, 2, 4):
        shift = 8 - d if reverse else d
        a_s = pltpu.roll(a, shift, 0)
        u_s = pltpu.roll(u, shift, 0)
        m = (row < 8 - d) if reverse else (row >= d)
        u = jnp.where(m, a * u_s + u, u)
        a = jnp.where(m, a * a_s, a)
    return a, u


def _lru_kernel(x_ref, xp_ref, xn_ref, cw_ref, cb_ref, gw_ref, gb_ref, lam_ref, h0_ref, out_ref, st_ref,
                af, uf, ab, ub, gg, *, seq, tr):
    j = pl.program_id(1)
    nt = seq // tr
    xg = x_ref[...]
    x = xg[:, 0:LRU_WIDTH]
    prev = jnp.where(j > 0, xp_ref[...], 0.0)
    nxt = jnp.where(j < nt - 1, xn_ref[...], 0.0)
    xe = jnp.concatenate([prev, x, nxt], axis=0)
    ne = tr + 16
    cw = cw_ref[...]
    xc = (cw[0:1] * pltpu.roll(xe, 2, 0)[8:8 + tr] + cw[1:2] * pltpu.roll(xe, 1, 0)[8:8 + tr]
          + cw[2:3] * x + cw[3:4] * pltpu.roll(xe, ne - 1, 0)[8:8 + tr] + cb_ref[...])
    sg = jax.nn.sigmoid(jnp.dot(xc.astype(BF16), gw_ref[...], preferred_element_type=F32) + gb_ref[...])
    z = -lam_ref[...]
    neg_c_softplus = -LRU_C * (jnp.maximum(z, 0.0) + jnp.log1p(jnp.exp(-jnp.abs(z))))
    r0 = pl.multiple_of(j * tr, tr)
    for d, (a_s, u_s) in enumerate(((af, uf), (ab, ub))):
        r = sg[:, 512 * d:512 * d + LRU_WIDTH]
        i = sg[:, 512 * d + LRU_WIDTH:512 * (d + 1)]
        log_a = neg_c_softplus[d:d + 1] * r
        a = jnp.exp(log_a)
        a_s[pl.ds(r0, tr), :] = a
        u_s[pl.ds(r0, tr), :] = jnp.sqrt(-jnp.tanh(log_a) * (a * a + 1.0)) * (i * xc)
    gg[pl.ds(r0, tr), :] = jax.nn.gelu(xg[:, LRU_WIDTH:2 * LRU_WIDTH])

    @pl.when(j == nt - 1)
    def _():
        row = lax.broadcasted_iota(jnp.int32, (8, LRU_WIDTH), 0)
        nchunk = seq // 8

        def body(c, carry):
            hf, hb = carry
            rf = pl.multiple_of(c * 8, 8)
            a, u = _chunk_scan(af[pl.ds(rf, 8), :], uf[pl.ds(rf, 8), :], row, False)
            hs = u + a * hf
            uf[pl.ds(rf, 8), :] = hs
            rb = pl.multiple_of((nchunk - 1 - c) * 8, 8)
            a2, u2 = _chunk_scan(ab[pl.ds(rb, 8), :], ub[pl.ds(rb, 8), :], row, True)
            hs2 = u2 + a2 * hb
            ub[pl.ds(rb, 8), :] = hs2
            return hs[7:8, :], hs2[0:1, :]

        h0 = h0_ref[...]
        hf, hb = lax.fori_loop(0, nchunk, body, (h0[0:1], h0[1:2]))
        st_ref[...] = jnp.concatenate([hf, hb], axis=0)

        def obody(t, carry):
            r = pl.multiple_of(t * tr, tr)
            out_ref[pl.ds(r, tr), :] = (gg[pl.ds(r, tr), :] * (uf[pl.ds(r, tr), :] + ub[pl.ds(r, tr), :])).astype(BF16)
            return carry

        lax.fori_loop(0, nt, obody, 0)


def _lru(lru, cw, cb, gw, gb, lam, h0, *, seq, tr):
    n = lru.shape[0]
    b = n // seq
    nt = seq // tr
    nb8 = n // 8
    const = lambda bi, j: (0, 0)
    in_specs = [pl.BlockSpec((tr, 512), lambda bi, j: (bi * nt + j, 0)),
                pl.BlockSpec((8, LRU_WIDTH), lambda bi, j: (jnp.maximum((bi * nt + j) * (tr // 8) - 1, 0), 0)),
                pl.BlockSpec((8, LRU_WIDTH), lambda bi, j: (jnp.minimum((bi * nt + j + 1) * (tr // 8), nb8 - 1), 0)),
                pl.BlockSpec((4, LRU_WIDTH), const),
                pl.BlockSpec((1, LRU_WIDTH), const),
                pl.BlockSpec((LRU_WIDTH, 4 * LRU_WIDTH), const),
                pl.BlockSpec((1, 4 * LRU_WIDTH), const),
                pl.BlockSpec((2, LRU_WIDTH), const),
                pl.BlockSpec((None, 2, LRU_WIDTH), lambda bi, j: (bi, 0, 0))]
    out_specs = [pl.BlockSpec((seq, LRU_WIDTH), lambda bi, j: (bi, 0)),
                 pl.BlockSpec((None, 2, LRU_WIDTH), lambda bi, j: (bi, 0, 0))]
    out_shape = [jax.ShapeDtypeStruct((n, LRU_WIDTH), BF16),
                 jax.ShapeDtypeStruct((b, 2, LRU_WIDTH), F32)]
    return pl.pallas_call(
        functools.partial(_lru_kernel, seq=seq, tr=tr),
        grid=(b, nt), in_specs=in_specs, out_specs=out_specs, out_shape=out_shape,
        scratch_shapes=[pltpu.VMEM((seq, LRU_WIDTH), F32)] * 5,
        compiler_params=_params("arbitrary", "arbitrary"), name="lru",
    )(lru, lru, lru, cw, cb, gw, gb, lam, h0)


_NT = (((1,), (1,)), ((), ()))


def _attn_kernel(*refs, n_ctx, seq, tq, pack, lam_init):
    if n_ctx:
        (q_ref, k_ref, v_ref, dk_ref, dv_ref, ck_ref, cv_ref, cdk_ref, cdv_ref, dl_ref, dg_ref,
         gqa_ref, dout_ref, ks, vts, s_buf0, s_buf1, e_buf0, e_buf1, o_buf) = refs
    else:
        (q_ref, k_ref, v_ref, dk_ref, dv_ref, dl_ref, dg_ref,
         gqa_ref, dout_ref, ks, vts, s_buf0, s_buf1, e_buf0, e_buf1, o_buf) = refs
    s_bufs = (s_buf0, s_buf1)
    e_bufs = (e_buf0, e_buf1)
    t_all = n_ctx + seq
    n_chunks = t_all // KEY_CHUNK

    def put(slab, new, ctx):
        if n_ctx:
            ks[slab, 0:n_ctx, :] = ctx[0].astype(BF16)
            vts[slab, :, 0:n_ctx] = ctx[1].T.astype(BF16)
        ks[slab, n_ctx:t_all, :] = new[0].astype(BF16)
        vts[slab, :, n_ctx:t_all] = new[1].T.astype(BF16)

    put(0, (k_ref[...], v_ref[...]), (ck_ref[...], cv_ref[...]) if n_ctx else None)
    for c in range(2):
        cols = slice(LANES * c, LANES * (c + 1))
        put(1 + c, (dk_ref[:, cols], dv_ref[:, cols]), (cdk_ref[:, cols], cdv_ref[:, cols]) if n_ctx else None)

    width = pack * tq
    upb = N_JOBS // pack
    n_units = upb * (seq // tq)

    def slab_of(t):
        g = (t % upb) * pack
        return jnp.where(g < GQA_HEADS, 0, 1 + (g - GQA_HEADS) // 4)

    def q_unit(t):
        u, qb = t % upb, t // upb
        r0 = qb * tq if isinstance(t, int) else pl.multiple_of(qb * tq, tq)
        if pack == 1:
            return q_ref[u, pl.ds(r0, tq), :]
        return q_ref[pl.ds(u * pack, pack), pl.ds(r0, tq), :].reshape(width, LANES)

    def fold8(x, op):
        acc = x[0:8]
        for r in range(1, KEY_CHUNK // 8):
            acc = op(acc, x[8 * r:8 * (r + 1)])
        return acc

    def run(t, par, m_prev, l_prev, do_a=True, do_b=True, do_c=True):
        ga, gb, gc = t, t - 2, t - 4
        s_buf, e_buf = s_bufs[par], e_bufs[par]
        m_acc = l_acc = None
        if do_a:
            qa = q_unit(ga)
            ka = slab_of(ga)
            m_acc = jnp.full((8, width), -jnp.inf, F32)
        if do_b:
            m_row = jnp.max(m_prev, axis=0, keepdims=True)
            l_acc = jnp.zeros((8, width), F32)
        if do_c:
            vc = slab_of(gc)
            o_acc = jnp.zeros((LANES, width), F32)
        for c in range(n_chunks):
            rows = slice(KEY_CHUNK * c, KEY_CHUNK * (c + 1))
            if do_c:
                o_acc = o_acc + jnp.dot(vts[vc, :, rows], e_buf[rows, :], preferred_element_type=F32)
            if do_b:
                e = jnp.exp2(s_buf[rows, :] - m_row)
                e_buf[rows, :] = e.astype(BF16)
                l_acc = l_acc + fold8(e, jnp.add)
            if do_a:
                s = lax.dot_general(ks[ka, rows, :], qa, _NT, preferred_element_type=F32)
                s_buf[rows, :] = s
                m_acc = jnp.maximum(m_acc, fold8(s, jnp.maximum))
        if do_c:
            o_buf[gc % upb] = o_acc * (1.0 / jnp.sum(l_prev, axis=0, keepdims=True))
        return m_acc, l_acc

    lane = lax.broadcasted_iota(jnp.int32, (1, LANES), 1)
    low_half = lane < HEAD_DIM

    def job_out(g, rows):
        return o_buf[g // pack, rows, tq * (g % pack):tq * (g % pack + 1)]

    def finish_block(qb):
        r0 = qb * tq if isinstance(qb, int) else pl.multiple_of(qb * tq, tq)
        out_rows = pl.ds(r0, tq)
        for c in range(4):
            h = c // 2
            rows = slice(HEAD_DIM * h, HEAD_DIM * (h + 1))
            ot = jnp.concatenate([job_out(2 * c, rows), job_out(2 * c + 1, rows)], axis=0)
            gqa_ref[out_rows, LANES * c:LANES * (c + 1)] = ot.T.astype(BF16)

        dl = dl_ref[...]
        lam = (jnp.exp(jnp.sum(dl[0:1] * dl[1:2], axis=-1, keepdims=True))
               - jnp.exp(jnp.sum(dl[2:3] * dl[3:4], axis=-1, keepdims=True)) + lam_init)

        for c in range(2):
            parts = []
            for par in range(2):
                g1 = GQA_HEADS + 2 * (2 * c + par)
                rows = slice(HEAD_DIM * par, HEAD_DIM * (par + 1))
                parts.append(job_out(g1, rows) - lam * job_out(g1 + 1, rows))
            o = jnp.concatenate(parts, axis=0).T
            sq = o * o
            ms = jnp.where(low_half,
                           jnp.sum(jnp.where(low_half, sq, 0.0), axis=-1, keepdims=True),
                           jnp.sum(jnp.where(low_half, 0.0, sq), axis=-1, keepdims=True)) * (1.0 / HEAD_DIM)
            dout_ref[out_rows, LANES * c:LANES * (c + 1)] = (
                o * lax.rsqrt(ms + EPS) * dg_ref[...] * (1.0 - lam_init)).astype(BF16)

    bpb = upb // 2
    n_bodies = n_units // 2

    def body(i, carry, **stages):
        (m0, m1), (l0, l1) = carry
        m0, l0 = run(2 * i, 0, m0, l0, **stages)
        m1, l1 = run(2 * i + 1, 1, m1, l1, **stages)
        if isinstance(i, int):
            if i > 1 and (i - 1) % bpb == 0:
                finish_block((i - 1) // bpb - 1)
        elif n_bodies > bpb:
            @pl.when((i - 1) % bpb == 0)
            def _():
                finish_block((i - 1) // bpb - 1)
        return (m0, m1), (l0, l1)

    carry = body(0, ((None, None), (None, None)), do_b=False, do_c=False)
    carry = body(1, (carry[0], (None, None)), do_c=False)
    carry = lax.fori_loop(2, n_bodies, body, carry)
    carry = body(n_bodies, carry, do_a=False)
    body(n_bodies + 1, ((None, None), carry[1]), do_a=False, do_b=False)


def _attention(qx, k, v, dk, dv, caches, dl, dg, *, layer, seq, tq, lam_init):
    pack = ATTN_WIDTH // tq
    n = k.shape[0]
    b = n // seq
    n_ctx = 0 if caches is None else caches[0].shape[2]
    t_all = n_ctx + seq
    per_b = lambda bi: (bi, 0)
    const = lambda bi: (0, 0)
    once = pl.Buffered(1)
    in_specs = [pl.BlockSpec((N_JOBS, seq, LANES), lambda bi: (0, bi, 0)),
                pl.BlockSpec((seq, 128), per_b, pipeline_mode=once),
                pl.BlockSpec((seq, 128), per_b, pipeline_mode=once),
                pl.BlockSpec((seq, 256), per_b, pipeline_mode=once),
                pl.BlockSpec((seq, 256), per_b, pipeline_mode=once)]
    args = [qx, k, v, dk, dv]
    if n_ctx:
        cache_idx = lambda bi: (bi, layer, 0, 0)
        in_specs += [pl.BlockSpec((None, None, n_ctx, 128), cache_idx),
                     pl.BlockSpec((None, None, n_ctx, 128), cache_idx),
                     pl.BlockSpec((None, None, n_ctx, 256), cache_idx),
                     pl.BlockSpec((None, None, n_ctx, 256), cache_idx)]
        args += list(caches)
    in_specs += [pl.BlockSpec((4, DIFF_QK_DIM), const), pl.BlockSpec((1, LANES), const)]
    args += [dl, dg]
    return pl.pallas_call(
        functools.partial(_attn_kernel, n_ctx=n_ctx, seq=seq, tq=tq, pack=pack, lam_init=lam_init),
        grid=(b,), in_specs=in_specs,
        out_specs=[pl.BlockSpec((seq, 512), per_b), pl.BlockSpec((seq, 256), per_b)],
        out_shape=[jax.ShapeDtypeStruct((n, 512), BF16), jax.ShapeDtypeStruct((n, 256), BF16)],
        scratch_shapes=[pltpu.VMEM((3, t_all, LANES), BF16), pltpu.VMEM((3, LANES, t_all), BF16),
                        pltpu.VMEM((t_all, ATTN_WIDTH), F32), pltpu.VMEM((t_all, ATTN_WIDTH), F32),
                        pltpu.VMEM((t_all, ATTN_WIDTH), BF16), pltpu.VMEM((t_all, ATTN_WIDTH), BF16),
                        pltpu.VMEM((N_JOBS // pack, LANES, ATTN_WIDTH), F32)],
        compiler_params=_params("arbitrary"), name="attention",
    )(*args)


def _out_proj_kernel(lru_ref, gqa_ref, d_ref, w_ref, x_ref, mod_ref, n2_ref, x1_ref, h2_ref):
    m = (jnp.dot(lru_ref[...], w_ref[0:256, :], preferred_element_type=F32)
         + jnp.dot(gqa_ref[...], w_ref[256:768, :], preferred_element_type=F32)
         + jnp.dot(d_ref[...], w_ref[768:1024, :], preferred_element_type=F32))
    mod = mod_ref[...]
    x1 = x_ref[...] + mod[:, 2 * D_MODEL:3 * D_MODEL] * m
    x1_ref[...] = x1
    h2_ref[...] = (_rms(x1) * (n2_ref[...] * (1.0 + mod[:, 4 * D_MODEL:5 * D_MODEL]))
                   + mod[:, 3 * D_MODEL:4 * D_MODEL]).astype(BF16)


def _out_proj(lru_o, gqa_o, d_o, w_out, x, mod, n2, *, layer, seq, tm, mod_row):
    n = x.shape[0]
    nts = seq // tm
    row = lambda i: (i, 0)
    const = lambda i: (0, 0)
    return pl.pallas_call(
        _out_proj_kernel,
        grid=(n // tm,),
        in_specs=[pl.BlockSpec((tm, 256), row), pl.BlockSpec((tm, 512), row), pl.BlockSpec((tm, 256), row),
                  pl.BlockSpec((None, D_MODEL, D_MODEL), lambda i: (layer, 0, 0)),
                  pl.BlockSpec((tm, D_MODEL), row),
                  pl.BlockSpec((None, 1, 6 * D_MODEL), lambda i: (mod_row(i // nts), 0, 0)),
                  pl.BlockSpec((1, D_MODEL), const)],
        out_specs=[pl.BlockSpec((tm, D_MODEL), row), pl.BlockSpec((tm, D_MODEL), row)],
        out_shape=[jax.ShapeDtypeStruct((n, D_MODEL), F32), jax.ShapeDtypeStruct((n, D_MODEL), BF16)],
        compiler_params=_params("arbitrary"), name="out_proj",
    )(lru_o, gqa_o, d_o, w_out, x, mod, n2)


HALO = 16


def _ffn_kernel(h_ref, hp_ref, hn_ref, x1_ref, up_ref, cw_ref, cb_ref, wd_ref,
                mod_ref, fin_ref, o_ref, he, acc, *, seq, tm, final):
    i = pl.program_id(0)
    nts = seq // tm
    pos = i % nts

    @pl.when(pos == 0)
    def _():
        he[0:HALO, :] = jnp.zeros((HALO, D_MODEL), BF16)

    @pl.when(pos != 0)
    def _():
        he[0:HALO, :] = hp_ref[...]

    @pl.when(pos == nts - 1)
    def _():
        he[HALO + tm:2 * HALO + tm, :] = jnp.zeros((HALO, D_MODEL), BF16)

    @pl.when(pos != nts - 1)
    def _():
        he[HALO + tm:2 * HALO + tm, :] = hn_ref[...]

    he[HALO:HALO + tm, :] = h_ref[...]
    ne = tm + 2 * HALO

    def up(c, half):
        cols = slice(half * D_FF + FFN_CHUNK * c, half * D_FF + FFN_CHUNK * (c + 1))
        u = jnp.dot(he[...], up_ref[:, cols], preferred_element_type=F32)
        cw = cw_ref[:, cols]
        return (cw[0:1] * pltpu.roll(u, 1, 0)[HALO:HALO + tm] + cw[1:2] * u[HALO:HALO + tm]
                + cw[2:3] * pltpu.roll(u, ne - 1, 0)[HALO:HALO + tm] + cb_ref[:, cols])

    nxt = (up(0, 0), up(0, 1))
    for c in range(N_FFN_CHUNKS):
        a, g = nxt
        if c + 1 < N_FFN_CHUNKS:
            nxt = (up(c + 1, 0), up(c + 1, 1))
        act = (g * jax.nn.sigmoid(g) * a).astype(BF16)
        d = jnp.dot(act, wd_ref[FFN_CHUNK * c:FFN_CHUNK * (c + 1), :], preferred_element_type=F32)
        if c == 0:
            acc[...] = d
        else:
            acc[...] += d
    mod = mod_ref[...]
    x2 = x1_ref[...] + mod[:, 5 * D_MODEL:6 * D_MODEL] * acc[...]
    if final:
        x2 = _rms(x2) * fin_ref[...]
    o_ref[...] = x2


def _conv_ffn(h2, x1, up, cw, cb, wd, mod, fin_g, *, layer, seq, tm, mod_row, final):
    n = x1.shape[0]
    nts = seq // tm
    nbh = n // HALO
    row = lambda i: (i, 0)
    c2 = lambda i: (0, 0)
    per_layer = lambda i: (layer, 0, 0)
    in_specs = [pl.BlockSpec((tm, D_MODEL), row),
                pl.BlockSpec((HALO, D_MODEL), lambda i: (jnp.maximum(i * (tm // HALO) - 1, 0), 0)),
                pl.BlockSpec((HALO, D_MODEL), lambda i: (jnp.minimum((i + 1) * (tm // HALO), nbh - 1), 0)),
                pl.BlockSpec((tm, D_MODEL), row),
                pl.BlockSpec((None, D_MODEL, 2 * D_FF), per_layer),
                pl.BlockSpec((None, 3, 2 * D_FF), per_layer),
                pl.BlockSpec((None, 1, 2 * D_FF), per_layer),
                pl.BlockSpec((None, D_FF, D_MODEL), per_layer),
                pl.BlockSpec((None, 1, 6 * D_MODEL), lambda i: (mod_row(i // nts), 0, 0)),
                pl.BlockSpec((1, D_MODEL), c2)]
    return pl.pallas_call(
        functools.partial(_ffn_kernel, seq=seq, tm=tm, final=final),
        grid=(n // tm,), in_specs=in_specs,
        out_specs=pl.BlockSpec((tm, D_MODEL), row),
        out_shape=jax.ShapeDtypeStruct((n, D_MODEL), F32),
        scratch_shapes=[pltpu.VMEM((tm + 2 * HALO, D_MODEL), BF16), pltpu.VMEM((tm, D_MODEL), F32)],
        compiler_params=_params("arbitrary"), name="conv_ffn",
    )(h2, h2, h2, x1, up, cw, cb, wd, mod, fin_g)


def _rope_tables(seq, dim):
    rows = seq // GRID_W
    t_row = np.repeat(np.arange(rows, dtype=np.float64), GRID_W)
    t_col = np.tile(np.arange(GRID_W, dtype=np.float64), rows)
    axis_dim = dim // 2
    inv = ROPE_THETA ** (-np.arange(0, axis_dim, 2, dtype=np.float64) / axis_dim)
    ar = t_row[:, None] * inv
    ac = t_col[:, None] * inv
    ang = np.concatenate([ar, ar, ac, ac], axis=-1)
    reps = LANES // dim
    cos = np.tile(np.cos(ang), (1, reps))
    sin = np.tile(np.sin(ang), (1, reps))
    first = (np.arange(LANES) % (dim // 2)) < (dim // 4)
    sin_neg = np.where(first, -sin, 0.0)
    sin_pos = np.where(first, 0.0, sin)
    return tuple(jnp.asarray(t, F32) for t in (cos, sin_neg, sin_pos))


def _gate_weights(w, b):
    eye = jnp.eye(LRU_BLOCKS, dtype=w.dtype)
    cols, biases = [], []
    for d in range(2):
        for part in range(2):
            blk = w[d, :, :, part * LRU_BLOCK_W:(part + 1) * LRU_BLOCK_W]
            cols.append(jnp.einsum('nde,nm->ndme', blk, eye).reshape(LRU_WIDTH, LRU_WIDTH))
            biases.append(b[d, :, part * LRU_BLOCK_W:(part + 1) * LRU_BLOCK_W].reshape(LRU_WIDTH))
    return jnp.concatenate(cols, axis=1).astype(BF16), jnp.concatenate(biases)[None, :]


def kernel(x_prompt, x_sample, cache_gqa_k, cache_gqa_v, cache_diff_k, cache_diff_v, state_lru, c, c_ctx,
           norm1_g, norm2_g, final_norm_g, ada_w, ada_b, w_in, w_out, lru_conv_w, lru_conv_b, lru_gate_w,
           lru_gate_b, lru_lambda, gqa_q_norm_g, gqa_k_norm_g, diff_lambda, diff_norm_g, ffn_w_up, ffn_conv_w,
           ffn_conv_b, ffn_w_down):
    bp, sp, _ = x_prompt.shape
    bs, ss, _ = x_sample.shape
    n_ctx = cache_gqa_k.shape[2]

    cond = jnp.concatenate([c_ctx[None, :], c, jnp.zeros((COND_ROWS - 1 - bs, D_MODEL), F32)], axis=0)
    mod_all = _ada(cond, ada_w, ada_b)

    caches = (cache_gqa_k.reshape(bs, DEPTH, n_ctx, 128), cache_gqa_v.reshape(bs, DEPTH, n_ctx, 128),
              cache_diff_k.reshape(bs, DEPTH, n_ctx, 256), cache_diff_v.reshape(bs, DEPTH, n_ctx, 256))
    tables = _rope_tables(ss, HEAD_DIM) + _rope_tables(ss, DIFF_QK_DIM)
    ones_blk = jnp.asarray(np.kron(np.eye(2), np.ones((HEAD_DIM, HEAD_DIM))), BF16)
    zero_state = jnp.zeros((bp, 2, LRU_WIDTH), F32)

    groups = {
        'p': dict(seq=sp, tm=256, tq=256, mod_row=lambda b: 0, tables=None, caches=None),
        's': dict(seq=ss, tm=512, tq=512, mod_row=lambda b: 1 + b, tables=tables, caches=caches),
    }
    xs = {'p': x_prompt.reshape(bp * sp, D_MODEL), 's': x_sample.reshape(bs * ss, D_MODEL)}
    new_k, new_v, new_dk, new_dv, new_st = [], [], [], [], []

    w_in_b = w_in.astype(BF16)
    w_out_b = w_out.astype(BF16)
    up_b = ffn_w_up.astype(BF16)
    wd_b = ffn_w_down.astype(BF16)
    ffn_cb = ffn_conv_b.reshape(DEPTH, 1, 2 * D_FF)
    fin = final_norm_g[None, :]

    for l in range(DEPTH):
        lam_init = 0.8 - 0.6 * math.exp(-0.3 * l)
        mod = mod_all[l].reshape(COND_ROWS, 1, 6 * D_MODEL)
        gw, gb = _gate_weights(lru_gate_w[l], lru_gate_b[l])
        qg = jnp.tile(gqa_q_norm_g[l], 2)[None, :]
        kg = jnp.tile(gqa_k_norm_g[l], 2)[None, :]
        dg = jnp.tile(diff_norm_g[l], 2)[None, :]
        n1 = norm1_g[l][None, :]
        n2 = norm2_g[l][None, :]

        for name in ('p', 's'):
            g = groups[name]
            seq, tm, mod_row = g['seq'], g['tm'], g['mod_row']
            x = xs[name]
            lru, qx, k, v, dk, dv = _norm_proj(x, mod, n1, w_in_b, qg, kg, ones_blk, g['tables'],
                                               layer=l, seq=seq, tm=tm, mod_row=mod_row)
            h0 = zero_state if name == 'p' else state_lru[:, l]
            lru_o, st = _lru(lru, lru_conv_w[l], lru_conv_b[l][None, :], gw, gb, lru_lambda[l], h0, seq=seq, tr=256)
            gqa_o, d_o = _attention(qx, k, v, dk, dv, g['caches'], diff_lambda[l], dg,
                                    layer=l, seq=seq, tq=g['tq'], lam_init=lam_init)
            x1, h2 = _out_proj(lru_o, gqa_o, d_o, w_out_b, x, mod, n2, layer=l, seq=seq, tm=tm, mod_row=mod_row)
            xs[name] = _conv_ffn(h2, x1, up_b, ffn_conv_w, ffn_cb, wd_b, mod, fin,
                                 layer=l, seq=seq, tm=tm, mod_row=mod_row, final=(l == DEPTH - 1))
            if name == 'p':
                new_k.append(k)
                new_v.append(v)
                new_dk.append(dk)
                new_dv.append(dv)
                new_st.append(st)

    y_prompt = xs['p'].reshape(bp, sp, D_MODEL)
    y_sample = xs['s'].reshape(bs, ss, D_MODEL)
    stack = lambda parts, shape: jnp.stack([p.reshape((bp,) + shape) for p in parts], axis=1)
    return (y_prompt, y_sample,
            stack(new_k, (sp, GQA_KV_HEADS, HEAD_DIM)),
            stack(new_v, (sp, GQA_KV_HEADS, HEAD_DIM)),
            stack(new_dk, (sp, DIFF_HEADS, 2, DIFF_QK_DIM)),
            stack(new_dv, (sp, DIFF_HEADS, HEAD_DIM)),
            jnp.stack(new_st, axis=1))
```

```python
import functools
import math

import numpy as np
import jax
import jax.numpy as jnp
from jax import lax
from jax.experimental import pallas as pl
from jax.experimental.pallas import tpu as pltpu

F32 = jnp.float32
BF16 = jnp.bfloat16

D_MODEL = 1024
DEPTH = 2
GRID_W = 64
HEAD_DIM = 64
LRU_WIDTH = 256
LRU_BLOCKS = 4
LRU_BLOCK_W = LRU_WIDTH // LRU_BLOCKS
LRU_C = 8.0
GQA_HEADS = 8
GQA_KV_HEADS = 2
DIFF_HEADS = 4
DIFF_QK_DIM = 32
D_FF = 2816
ROPE_THETA = 10000.0
EPS = 1e-6

LOG2_E = math.log2(math.e)
N_JOBS = GQA_HEADS + 2 * DIFF_HEADS
KEY_CHUNK = 256
ATTN_WIDTH = 512
LANES = 128
FFN_CHUNK = 256
N_FFN_CHUNKS = D_FF // FFN_CHUNK
COND_ROWS = 8
VMEM_LIMIT = 56 * 2 ** 20


def _params(*sem):
    return pltpu.CompilerParams(dimension_semantics=sem, vmem_limit_bytes=VMEM_LIMIT)


def _rms(x):
    return x * lax.rsqrt(jnp.mean(x * x, axis=-1, keepdims=True) + EPS)


def _ada_kernel(cond_ref, w_ref, b_ref, o_ref):
    c = cond_ref[...]
    s = c * jax.nn.sigmoid(c)
    o_ref[...] = jnp.dot(s.astype(BF16), w_ref[...].astype(BF16), preferred_element_type=F32) + b_ref[...]


def _ada(cond, ada_w, ada_b):
    tn = 1536
    width = 6 * D_MODEL
    return pl.pallas_call(
        _ada_kernel,
        grid=(DEPTH, width // tn),
        in_specs=[pl.BlockSpec((COND_ROWS, D_MODEL), lambda l, j: (0, 0)),
                  pl.BlockSpec((None, D_MODEL, tn), lambda l, j: (l, 0, j)),
                  pl.BlockSpec((None, 1, tn), lambda l, j: (l, 0, j))],
        out_specs=pl.BlockSpec((None, COND_ROWS, tn), lambda l, j: (l, 0, j)),
        out_shape=jax.ShapeDtypeStruct((DEPTH, COND_ROWS, width), F32),
        compiler_params=_params("arbitrary", "arbitrary"),
        name="ada",
    )(cond, ada_w, ada_b.reshape(DEPTH, 1, width))


def _group_mean_sq(x, ones_blk):
    sq = x * x
    hi = sq.astype(BF16)
    lo = (sq - hi.astype(F32)).astype(BF16)
    s = jnp.dot(hi, ones_blk, preferred_element_type=F32) + jnp.dot(lo, ones_blk, preferred_element_type=F32)
    return s * (1.0 / HEAD_DIM)


def _rope(x, cos, sin_neg, sin_pos, quarter):
    return x * cos + pltpu.roll(x, LANES - quarter, 1) * sin_neg + pltpu.roll(x, quarter, 1) * sin_pos


def _norm_proj_kernel(*refs, rope):
    if rope:
        (x_ref, mod_ref, n1_ref, w_ref, qg_ref, kg_ref, ones_ref, cq, snq, spq, cd, snd, spd,
         lru_ref, qx_ref, k_ref, v_ref, dk_ref, dv_ref) = refs
    else:
        (x_ref, mod_ref, n1_ref, w_ref, qg_ref, kg_ref, ones_ref,
         lru_ref, qx_ref, k_ref, v_ref, dk_ref, dv_ref) = refs
    mod = mod_ref[...]
    h = _rms(x_ref[...]) * (n1_ref[...] * (1.0 + mod[:, D_MODEL:2 * D_MODEL])) + mod[:, 0:D_MODEL]
    proj = jnp.dot(h.astype(BF16), w_ref[...], preferred_element_type=F32)
    lru_ref[...] = proj[:, 0:512]
    ones_blk = ones_ref[...]
    lane = lax.broadcasted_iota(jnp.int32, (1, LANES), 1)

    for c in range(4):
        xc = proj[:, 512 + LANES * c:512 + LANES * (c + 1)]
        xc = xc * lax.rsqrt(_group_mean_sq(xc, ones_blk) + EPS) * qg_ref[...]
        if rope:
            xc = _rope(xc, cq[...], snq[...], spq[...], HEAD_DIM // 4)
        xc = xc * (HEAD_DIM ** -0.5 * LOG2_E)
        xr = pltpu.roll(xc, HEAD_DIM, 1)
        for par in range(2):
            j = 2 * c + par
            want = j // (GQA_HEADS // GQA_KV_HEADS)
            src = xc if par == want else xr
            qx_ref[j] = jnp.where(lane // HEAD_DIM == want, src, 0.0).astype(BF16)

    kc = proj[:, 1024:1152]
    kc = kc * lax.rsqrt(_group_mean_sq(kc, ones_blk) + EPS) * kg_ref[...]
    if rope:
        kc = _rope(kc, cq[...], snq[...], spq[...], HEAD_DIM // 4)
    k_ref[...] = kc
    v_ref[...] = proj[:, 1152:1280]

    for c in range(2):
        xc = proj[:, 1280 + LANES * c:1280 + LANES * (c + 1)]
        if rope:
            xc = _rope(xc, cd[...], snd[...], spd[...], DIFF_QK_DIM // 4)
        xc = xc * (DIFF_QK_DIM ** -0.5 * LOG2_E)
        for g in range(4):
            qx_ref[GQA_HEADS + 4 * c + g] = jnp.where(lane // DIFF_QK_DIM == g, xc, 0.0).astype(BF16)
        kc = proj[:, 1536 + LANES * c:1536 + LANES * (c + 1)]
        if rope:
            kc = _rope(kc, cd[...], snd[...], spd[...], DIFF_QK_DIM // 4)
        dk_ref[:, LANES * c:LANES * (c + 1)] = kc
    dv_ref[...] = proj[:, 1792:2048]


def _norm_proj(x, mod, n1, w_in, qg, kg, ones_blk, tables, *, layer, seq, tm, mod_row):
    n = x.shape[0]
    rope = tables is not None
    nts = seq // tm
    row = lambda i: (i, 0)
    const = lambda i: (0, 0)
    per_layer = lambda i: (layer, 0, 0)
    in_specs = [pl.BlockSpec((tm, D_MODEL), row),
                pl.BlockSpec((None, 1, 6 * D_MODEL), lambda i: (layer * COND_ROWS + mod_row(i // nts), 0, 0)),
                pl.BlockSpec((None, 1, D_MODEL), per_layer),
                pl.BlockSpec((None, D_MODEL, 2048), per_layer),
                pl.BlockSpec((None, 1, LANES), per_layer),
                pl.BlockSpec((None, 1, LANES), per_layer),
                pl.BlockSpec((LANES, LANES), const)]
    args = [x, mod, n1, w_in, qg, kg, ones_blk]
    if rope:
        in_specs += [pl.BlockSpec((tm, LANES), lambda i: (i % nts, 0))] * 6
        args += list(tables)
    slab = lambda i: (0, i, 0)
    out_specs = [pl.BlockSpec((tm, 512), row),
                 pl.BlockSpec((N_JOBS, tm, LANES), slab),
                 pl.BlockSpec((tm, 128), row),
                 pl.BlockSpec((tm, 128), row),
                 pl.BlockSpec((tm, 256), row),
                 pl.BlockSpec((tm, 256), row)]
    out_shape = [jax.ShapeDtypeStruct((n, 512), F32),
                 jax.ShapeDtypeStruct((N_JOBS, n, LANES), BF16),
                 jax.ShapeDtypeStruct((n, 128), F32),
                 jax.ShapeDtypeStruct((n, 128), F32),
                 jax.ShapeDtypeStruct((n, 256), F32),
                 jax.ShapeDtypeStruct((n, 256), F32)]
    return pl.pallas_call(
        functools.partial(_norm_proj_kernel, rope=rope),
        grid=(n // tm,), in_specs=in_specs, out_specs=out_specs, out_shape=out_shape,
        compiler_params=_params("arbitrary"), name="norm_proj",
    )(*args)


def _chunk_scan(a, u, row, reverse):
    for d in (1, 2, 4):
        shift = 8 - d if reverse else d
        a_s = pltpu.roll(a, shift, 0)
        u_s = pltpu.roll(u, shift, 0)
        m = (row < 8 - d) if reverse else (row >= d)
        u = jnp.where(m, a * u_s + u, u)
        a = jnp.where(m, a * a_s, a)
    return a, u


def _lru_kernel(x_ref, xp_ref, xn_ref, cw_ref, cb_ref, gw_ref, gb_ref, lam_ref, h0_ref, out_ref, st_ref,
                af, uf, ab, ub, gg, *, seq, tr):
    j = pl.program_id(1)
    nt = seq // tr
    xg = x_ref[...]
    x = xg[:, 0:LRU_WIDTH]
    prev = jnp.where(j > 0, xp_ref[...], 0.0)
    nxt = jnp.where(j < nt - 1, xn_ref[...], 0.0)
    xe = jnp.concatenate([prev, x, nxt], axis=0)
    ne = tr + 16
    cw = cw_ref[...]
    xc = (cw[0:1] * pltpu.roll(xe, 2, 0)[8:8 + tr] + cw[1:2] * pltpu.roll(xe, 1, 0)[8:8 + tr]
          + cw[2:3] * x + cw[3:4] * pltpu.roll(xe, ne - 1, 0)[8:8 + tr] + cb_ref[...])
    sg = jax.nn.sigmoid(jnp.dot(xc.astype(BF16), gw_ref[...], preferred_element_type=F32) + gb_ref[...])
    z = -lam_ref[...]
    neg_c_softplus = -LRU_C * (jnp.maximum(z, 0.0) + jnp.log1p(jnp.exp(-jnp.abs(z))))
    r0 = pl.multiple_of(j * tr, tr)
    for d, (a_s, u_s) in enumerate(((af, uf), (ab, ub))):
        r = sg[:, 512 * d:512 * d + LRU_WIDTH]
        i = sg[:, 512 * d + LRU_WIDTH:512 * (d + 1)]
        log_a = neg_c_softplus[d:d + 1] * r
        a = jnp.exp(log_a)
        a_s[pl.ds(r0, tr), :] = a
        u_s[pl.ds(r0, tr), :] = jnp.sqrt(-jnp.tanh(log_a) * (a * a + 1.0)) * (i * xc)
    gg[pl.ds(r0, tr), :] = jax.nn.gelu(xg[:, LRU_WIDTH:2 * LRU_WIDTH])

    @pl.when(j == nt - 1)
    def _():
        row = lax.broadcasted_iota(jnp.int32, (8, LRU_WIDTH), 0)
        nchunk = seq // 8

        def body(c, carry):
            hf, hb = carry
            rf = pl.multiple_of(c * 8, 8)
            a, u = _chunk_scan(af[pl.ds(rf, 8), :], uf[pl.ds(rf, 8), :], row, False)
            hs = u + a * hf
            uf[pl.ds(rf, 8), :] = hs
            rb = pl.multiple_of((nchunk - 1 - c) * 8, 8)
            a2, u2 = _chunk_scan(ab[pl.ds(rb, 8), :], ub[pl.ds(rb, 8), :], row, True)
            hs2 = u2 + a2 * hb
            ub[pl.ds(rb, 8), :] = hs2
            return hs[7:8, :], hs2[0:1, :]

        h0 = h0_ref[...]
        hf, hb = lax.fori_loop(0, nchunk, body, (h0[0:1], h0[1:2]))
        st_ref[...] = jnp.concatenate([hf, hb], axis=0)

        def obody(t, carry):
            r = pl.multiple_of(t * tr, tr)
            out_ref[pl.ds(r, tr), :] = (gg[pl.ds(r, tr), :] * (uf[pl.ds(r, tr), :] + ub[pl.ds(r, tr), :])).astype(BF16)
            return carry

        lax.fori_loop(0, nt, obody, 0)


def _lru(lru, cw, cb, gw, gb, lam, h0, *, layer, state_layer, seq, tr):
    n = lru.shape[0]
    b = n // seq
    nt = seq // tr
    nb8 = n // 8
    per_layer = lambda bi, j: (layer, 0, 0)
    in_specs = [pl.BlockSpec((tr, 512), lambda bi, j: (bi * nt + j, 0)),
                pl.BlockSpec((8, LRU_WIDTH), lambda bi, j: (jnp.maximum((bi * nt + j) * (tr // 8) - 1, 0), 0)),
                pl.BlockSpec((8, LRU_WIDTH), lambda bi, j: (jnp.minimum((bi * nt + j + 1) * (tr // 8), nb8 - 1), 0)),
                pl.BlockSpec((None, 4, LRU_WIDTH), per_layer),
                pl.BlockSpec((None, 1, LRU_WIDTH), per_layer),
                pl.BlockSpec((None, LRU_WIDTH, 4 * LRU_WIDTH), per_layer),
                pl.BlockSpec((None, 1, 4 * LRU_WIDTH), per_layer),
                pl.BlockSpec((None, 2, LRU_WIDTH), per_layer),
                pl.BlockSpec((None, None, 2, LRU_WIDTH), lambda bi, j: (bi, state_layer, 0, 0))]
    out_specs = [pl.BlockSpec((seq, LRU_WIDTH), lambda bi, j: (bi, 0)),
                 pl.BlockSpec((None, 2, LRU_WIDTH), lambda bi, j: (bi, 0, 0))]
    out_shape = [jax.ShapeDtypeStruct((n, LRU_WIDTH), BF16),
                 jax.ShapeDtypeStruct((b, 2, LRU_WIDTH), F32)]
    return pl.pallas_call(
        functools.partial(_lru_kernel, seq=seq, tr=tr),
        grid=(b, nt), in_specs=in_specs, out_specs=out_specs, out_shape=out_shape,
        scratch_shapes=[pltpu.VMEM((seq, LRU_WIDTH), F32)] * 5,
        compiler_params=_params("arbitrary", "arbitrary"), name="lru",
    )(lru, lru, lru, cw, cb, gw, gb, lam, h0)


_NT = (((1,), (1,)), ((), ()))


def _attn_kernel(*refs, n_ctx, seq, tq, pack, lam_init):
    if n_ctx:
        (q_ref, k_ref, v_ref, dk_ref, dv_ref, ck_ref, cv_ref, cdk_ref, cdv_ref, dl_ref, dg_ref,
         out_ref, ks, vts, s_buf0, s_buf1, e_buf0, e_buf1, o_buf) = refs
    else:
        (q_ref, k_ref, v_ref, dk_ref, dv_ref, dl_ref, dg_ref,
         out_ref, ks, vts, s_buf0, s_buf1, e_buf0, e_buf1, o_buf) = refs
    s_bufs = (s_buf0, s_buf1)
    e_bufs = (e_buf0, e_buf1)
    t_all = n_ctx + seq
    n_chunks = t_all // KEY_CHUNK

    def put(slab, new, ctx):
        if n_ctx:
            ks[slab, 0:n_ctx, :] = ctx[0].astype(BF16)
            vts[slab, :, 0:n_ctx] = ctx[1].T.astype(BF16)
        ks[slab, n_ctx:t_all, :] = new[0].astype(BF16)
        vts[slab, :, n_ctx:t_all] = new[1].T.astype(BF16)

    put(0, (k_ref[...], v_ref[...]), (ck_ref[...], cv_ref[...]) if n_ctx else None)
    for c in range(2):
        cols = slice(LANES * c, LANES * (c + 1))
        put(1 + c, (dk_ref[:, cols], dv_ref[:, cols]), (cdk_ref[:, cols], cdv_ref[:, cols]) if n_ctx else None)

    width = pack * tq
    upb = N_JOBS // pack
    n_units = upb * (seq // tq)

    def slab_of(t):
        g = (t % upb) * pack
        return jnp.where(g < GQA_HEADS, 0, 1 + (g - GQA_HEADS) // 4)

    def q_unit(t):
        u, qb = t % upb, t // upb
        r0 = qb * tq if isinstance(t, int) else pl.multiple_of(qb * tq, tq)
        if pack == 1:
            return q_ref[u, pl.ds(r0, tq), :]
        return q_ref[pl.ds(u * pack, pack), pl.ds(r0, tq), :].reshape(width, LANES)

    def fold8(x, op):
        acc = x[0:8]
        for r in range(1, KEY_CHUNK // 8):
            acc = op(acc, x[8 * r:8 * (r + 1)])
        return acc

    def run(t, par, m_prev, l_prev, do_a=True, do_b=True, do_c=True):
        ga, gb, gc = t, t - 2, t - 4
        s_buf, e_buf = s_bufs[par], e_bufs[par]
        m_acc = l_acc = None
        if do_a:
            qa = q_unit(ga)
            ka = slab_of(ga)
            m_acc = jnp.full((8, width), -jnp.inf, F32)
        if do_b:
            m_row = jnp.max(m_prev, axis=0, keepdims=True)
            l_acc = jnp.zeros((8, width), F32)
        if do_c:
            vc = slab_of(gc)
            o_acc = jnp.zeros((LANES, width), F32)
        for c in range(n_chunks):
            rows = slice(KEY_CHUNK * c, KEY_CHUNK * (c + 1))
            if do_c:
                o_acc = o_acc + jnp.dot(vts[vc, :, rows], e_buf[rows, :], preferred_element_type=F32)
            if do_b:
                e = jnp.exp2(s_buf[rows, :] - m_row)
                e_buf[rows, :] = e.astype(BF16)
                l_acc = l_acc + fold8(e, jnp.add)
            if do_a:
                s = lax.dot_general(ks[ka, rows, :], qa, _NT, preferred_element_type=F32)
                s_buf[rows, :] = s
                m_acc = jnp.maximum(m_acc, fold8(s, jnp.maximum))
        if do_c:
            o_buf[gc % upb] = o_acc * (1.0 / jnp.sum(l_prev, axis=0, keepdims=True))
        return m_acc, l_acc

    lane = lax.broadcasted_iota(jnp.int32, (1, LANES), 1)
    low_half = lane < HEAD_DIM

    def job_out(g, rows):
        return o_buf[g // pack, rows, tq * (g % pack):tq * (g % pack + 1)]

    def finish_block(qb):
        r0 = qb * tq if isinstance(qb, int) else pl.multiple_of(qb * tq, tq)
        out_rows = pl.ds(r0, tq)
        for c in range(4):
            h = c // 2
            rows = slice(HEAD_DIM * h, HEAD_DIM * (h + 1))
            ot = jnp.concatenate([job_out(2 * c, rows), job_out(2 * c + 1, rows)], axis=0)
            out_ref[out_rows, LANES * c:LANES * (c + 1)] = ot.T.astype(BF16)

        dl = dl_ref[...]
        lam = (jnp.exp(jnp.sum(dl[0:1] * dl[1:2], axis=-1, keepdims=True))
               - jnp.exp(jnp.sum(dl[2:3] * dl[3:4], axis=-1, keepdims=True)) + lam_init)

        for c in range(2):
            parts = []
            for par in range(2):
                g1 = GQA_HEADS + 2 * (2 * c + par)
                rows = slice(HEAD_DIM * par, HEAD_DIM * (par + 1))
                parts.append(job_out(g1, rows) - lam * job_out(g1 + 1, rows))
            o = jnp.concatenate(parts, axis=0).T
            sq = o * o
            ms = jnp.where(low_half,
                           jnp.sum(jnp.where(low_half, sq, 0.0), axis=-1, keepdims=True),
                           jnp.sum(jnp.where(low_half, 0.0, sq), axis=-1, keepdims=True)) * (1.0 / HEAD_DIM)
            out_ref[out_rows, GQA_HEADS * HEAD_DIM + LANES * c:GQA_HEADS * HEAD_DIM + LANES * (c + 1)] = (
                o * lax.rsqrt(ms + EPS) * dg_ref[...] * (1.0 - lam_init)).astype(BF16)

    bpb = upb // 2
    n_bodies = n_units // 2

    def body(i, carry, **stages):
        (m0, m1), (l0, l1) = carry
        m0, l0 = run(2 * i, 0, m0, l0, **stages)
        m1, l1 = run(2 * i + 1, 1, m1, l1, **stages)
        if isinstance(i, int):
            if i > 1 and (i - 1) % bpb == 0:
                finish_block((i - 1) // bpb - 1)
        elif n_bodies > bpb:
            @pl.when((i - 1) % bpb == 0)
            def _():
                finish_block((i - 1) // bpb - 1)
        return (m0, m1), (l0, l1)

    carry = body(0, ((None, None), (None, None)), do_b=False, do_c=False)
    carry = body(1, (carry[0], (None, None)), do_c=False)
    carry = lax.fori_loop(2, n_bodies, body, carry)
    carry = body(n_bodies, carry, do_a=False)
    body(n_bodies + 1, ((None, None), carry[1]), do_a=False, do_b=False)


def _attention(qx, k, v, dk, dv, caches, dl, dg, *, layer, seq, tq, lam_init):
    pack = ATTN_WIDTH // tq
    n = k.shape[0]
    b = n // seq
    n_ctx = 0 if caches is None else caches[0].shape[2]
    t_all = n_ctx + seq
    per_b = lambda bi: (bi, 0)
    once = pl.Buffered(1) if seq > ATTN_WIDTH else None
    in_specs = [pl.BlockSpec((N_JOBS, seq, LANES), lambda bi: (0, bi, 0)),
                pl.BlockSpec((seq, 128), per_b, pipeline_mode=once),
                pl.BlockSpec((seq, 128), per_b, pipeline_mode=once),
                pl.BlockSpec((seq, 256), per_b, pipeline_mode=once),
                pl.BlockSpec((seq, 256), per_b, pipeline_mode=once)]
    args = [qx, k, v, dk, dv]
    if n_ctx:
        cache_idx = lambda bi: (bi, layer, 0, 0)
        in_specs += [pl.BlockSpec((None, None, n_ctx, 128), cache_idx),
                     pl.BlockSpec((None, None, n_ctx, 128), cache_idx),
                     pl.BlockSpec((None, None, n_ctx, 256), cache_idx),
                     pl.BlockSpec((None, None, n_ctx, 256), cache_idx)]
        args += list(caches)
    in_specs += [pl.BlockSpec((None, 4, DIFF_QK_DIM), lambda bi: (layer, 0, 0)),
                 pl.BlockSpec((None, 1, LANES), lambda bi: (layer, 0, 0))]
    args += [dl, dg]
    return pl.pallas_call(
        functools.partial(_attn_kernel, n_ctx=n_ctx, seq=seq, tq=tq, pack=pack, lam_init=lam_init),
        grid=(b,), in_specs=in_specs,
        out_specs=pl.BlockSpec((seq, D_MODEL - LRU_WIDTH), per_b),
        out_shape=jax.ShapeDtypeStruct((n, D_MODEL - LRU_WIDTH), BF16),
        scratch_shapes=[pltpu.VMEM((3, t_all, LANES), BF16), pltpu.VMEM((3, LANES, t_all), BF16),
                        pltpu.VMEM((t_all, ATTN_WIDTH), F32), pltpu.VMEM((t_all, ATTN_WIDTH), F32),
                        pltpu.VMEM((t_all, ATTN_WIDTH), BF16), pltpu.VMEM((t_all, ATTN_WIDTH), BF16),
                        pltpu.VMEM((N_JOBS // pack, LANES, ATTN_WIDTH), F32)],
        compiler_params=_params("arbitrary"), name="attention",
    )(*args)


HALO = 16


def _mix_ffn_kernel(lru_ref, lrup_ref, lrun_ref, att_ref, attp_ref, attn_ref, x_ref, xp_ref, xn_ref,
                    wo_ref, n2_ref, up_ref, cw_ref, cb_ref, wd_ref, mod_ref, fin_ref, o_ref,
                    mix, xe, he, acc, *, seq, tm, final):
    i = pl.program_id(0)
    nts = seq // tm
    pos = i % nts
    ne = tm + 2 * HALO
    for r0, nr, lru_r, att_r, x_r in ((0, HALO, lrup_ref, attp_ref, xp_ref),
                                      (HALO, tm, lru_ref, att_ref, x_ref),
                                      (HALO + tm, HALO, lrun_ref, attn_ref, xn_ref)):
        mix[r0:r0 + nr, 0:LRU_WIDTH] = lru_r[...]
        mix[r0:r0 + nr, LRU_WIDTH:D_MODEL] = att_r[...]
        xe[r0:r0 + nr, :] = x_r[...]
    mod = mod_ref[...]
    x1 = xe[...] + mod[:, 2 * D_MODEL:3 * D_MODEL] * jnp.dot(mix[...], wo_ref[...], preferred_element_type=F32)
    xe[...] = x1
    he[...] = (_rms(x1) * (n2_ref[...] * (1.0 + mod[:, 4 * D_MODEL:5 * D_MODEL]))
               + mod[:, 3 * D_MODEL:4 * D_MODEL]).astype(BF16)

    @pl.when(pos == 0)
    def _():
        he[0:HALO, :] = jnp.zeros((HALO, D_MODEL), BF16)

    @pl.when(pos == nts - 1)
    def _():
        he[HALO + tm:ne, :] = jnp.zeros((HALO, D_MODEL), BF16)

    def up(c, half):
        cols = slice(half * D_FF + FFN_CHUNK * c, half * D_FF + FFN_CHUNK * (c + 1))
        u = jnp.dot(he[...], up_ref[:, cols], preferred_element_type=F32)
        cw = cw_ref[:, cols]
        return (cw[0:1] * pltpu.roll(u, 1, 0)[HALO:HALO + tm] + cw[1:2] * u[HALO:HALO + tm]
                + cw[2:3] * pltpu.roll(u, ne - 1, 0)[HALO:HALO + tm] + cb_ref[:, cols])

    nxt = (up(0, 0), up(0, 1))
    for c in range(N_FFN_CHUNKS):
        a, g = nxt
        if c + 1 < N_FFN_CHUNKS:
            nxt = (up(c + 1, 0), up(c + 1, 1))
        act = (g * jax.nn.sigmoid(g) * a).astype(BF16)
        d = jnp.dot(act, wd_ref[FFN_CHUNK * c:FFN_CHUNK * (c + 1), :], preferred_element_type=F32)
        if c == 0:
            acc[...] = d
        else:
            acc[...] += d
    x2 = xe[HALO:HALO + tm, :] + mod[:, 5 * D_MODEL:6 * D_MODEL] * acc[...]
    if final:
        x2 = _rms(x2) * fin_ref[...]
    o_ref[...] = x2


def _mix_ffn(lru_o, att_o, x, w_out, n2, up, cw, cb, wd, mod, fin_g, *, layer, seq, tm, mod_row, final):
    n = x.shape[0]
    nts = seq // tm
    nbh = n // HALO
    row = lambda i: (i, 0)
    prev = lambda i: (jnp.maximum(i * (tm // HALO) - 1, 0), 0)
    nxt = lambda i: (jnp.minimum((i + 1) * (tm // HALO), nbh - 1), 0)
    c2 = lambda i: (0, 0)
    per_layer = lambda i: (layer, 0, 0)
    att_w = D_MODEL - LRU_WIDTH

    def with_halo(width):
        return [pl.BlockSpec((tm, width), row), pl.BlockSpec((HALO, width), prev), pl.BlockSpec((HALO, width), nxt)]

    in_specs = (with_halo(LRU_WIDTH) + with_halo(att_w) + with_halo(D_MODEL)
                + [pl.BlockSpec((None, D_MODEL, D_MODEL), per_layer),
                   pl.BlockSpec((None, 1, D_MODEL), per_layer),
                   pl.BlockSpec((None, D_MODEL, 2 * D_FF), per_layer),
                   pl.BlockSpec((None, 3, 2 * D_FF), per_layer),
                   pl.BlockSpec((None, 1, 2 * D_FF), per_layer),
                   pl.BlockSpec((None, D_FF, D_MODEL), per_layer),
                   pl.BlockSpec((None, 1, 6 * D_MODEL), lambda i: (layer * COND_ROWS + mod_row(i // nts), 0, 0)),
                   pl.BlockSpec((1, D_MODEL), c2)])
    return pl.pallas_call(
        functools.partial(_mix_ffn_kernel, seq=seq, tm=tm, final=final),
        grid=(n // tm,), in_specs=in_specs,
        out_specs=pl.BlockSpec((tm, D_MODEL), row),
        out_shape=jax.ShapeDtypeStruct((n, D_MODEL), F32),
        scratch_shapes=[pltpu.VMEM((tm + 2 * HALO, D_MODEL), BF16), pltpu.VMEM((tm + 2 * HALO, D_MODEL), F32),
                        pltpu.VMEM((tm + 2 * HALO, D_MODEL), BF16), pltpu.VMEM((tm, D_MODEL), F32)],
        compiler_params=_params("arbitrary"), name="mix_ffn",
    )(lru_o, lru_o, lru_o, att_o, att_o, att_o, x, x, x, w_out, n2, up, cw, cb, wd, mod, fin_g)


def _rope_tables(seq, dim):
    rows = seq // GRID_W
    t_row = np.repeat(np.arange(rows, dtype=np.float64), GRID_W)
    t_col = np.tile(np.arange(GRID_W, dtype=np.float64), rows)
    axis_dim = dim // 2
    inv = ROPE_THETA ** (-np.arange(0, axis_dim, 2, dtype=np.float64) / axis_dim)
    ar = t_row[:, None] * inv
    ac = t_col[:, None] * inv
    ang = np.concatenate([ar, ar, ac, ac], axis=-1)
    reps = LANES // dim
    cos = np.tile(np.cos(ang), (1, reps))
    sin = np.tile(np.sin(ang), (1, reps))
    first = (np.arange(LANES) % (dim // 2)) < (dim // 4)
    sin_neg = np.where(first, -sin, 0.0)
    sin_pos = np.where(first, 0.0, sin)
    return tuple(jnp.asarray(t, F32) for t in (cos, sin_neg, sin_pos))


def _gate_weights(w, b):
    nl = w.shape[0]
    eye = jnp.eye(LRU_BLOCKS, dtype=w.dtype)
    wp = w.reshape(nl, 2, LRU_BLOCKS, LRU_BLOCK_W, 2, LRU_BLOCK_W)
    dense = jnp.einsum('lpndqe,nm->lndpqme', wp, eye).reshape(nl, LRU_WIDTH, 4 * LRU_WIDTH)
    bias = jnp.transpose(b.reshape(nl, 2, LRU_BLOCKS, 2, LRU_BLOCK_W), (0, 1, 3, 2, 4)).reshape(nl, 1, 4 * LRU_WIDTH)
    return dense.astype(BF16), bias


def kernel(x_prompt, x_sample, cache_gqa_k, cache_gqa_v, cache_diff_k, cache_diff_v, state_lru, c, c_ctx,
           norm1_g, norm2_g, final_norm_g, ada_w, ada_b, w_in, w_out, lru_conv_w, lru_conv_b, lru_gate_w,
           lru_gate_b, lru_lambda, gqa_q_norm_g, gqa_k_norm_g, diff_lambda, diff_norm_g, ffn_w_up, ffn_conv_w,
           ffn_conv_b, ffn_w_down):
    bp, sp, _ = x_prompt.shape
    bs, ss, _ = x_sample.shape
    n_ctx = cache_gqa_k.shape[2]

    cond = jnp.concatenate([c_ctx[None, :], c, jnp.zeros((COND_ROWS - 1 - bs, D_MODEL), F32)], axis=0)
    mod_all = _ada(cond, ada_w, ada_b)

    caches = (cache_gqa_k.reshape(bs, DEPTH, n_ctx, 128), cache_gqa_v.reshape(bs, DEPTH, n_ctx, 128),
              cache_diff_k.reshape(bs, DEPTH, n_ctx, 256), cache_diff_v.reshape(bs, DEPTH, n_ctx, 256))
    tables = _rope_tables(ss, HEAD_DIM) + _rope_tables(ss, DIFF_QK_DIM)
    ones_blk = jnp.asarray(np.kron(np.eye(2), np.ones((HEAD_DIM, HEAD_DIM))), BF16)
    zero_state = jnp.zeros((bp, 1, 2, LRU_WIDTH), F32)

    groups = {
        'p': dict(seq=sp, tm=256, tq=256, mod_row=lambda b: 0, tables=None, caches=None),
        's': dict(seq=ss, tm=512, tq=512, mod_row=lambda b: 1 + b, tables=tables, caches=caches),
    }
    xs = {'p': x_prompt.reshape(bp * sp, D_MODEL), 's': x_sample.reshape(bs * ss, D_MODEL)}
    new_k, new_v, new_dk, new_dv, new_st = [], [], [], [], []

    mod = mod_all.reshape(DEPTH * COND_ROWS, 1, 6 * D_MODEL)
    w_in_b = w_in.astype(BF16)
    w_out_b = w_out.astype(BF16)
    up_b = ffn_w_up.astype(BF16)
    wd_b = ffn_w_down.astype(BF16)
    ffn_cb = ffn_conv_b.reshape(DEPTH, 1, 2 * D_FF)
    fin = final_norm_g[None, :]
    gw, gb = _gate_weights(lru_gate_w, lru_gate_b)
    lru_cb = lru_conv_b.reshape(DEPTH, 1, LRU_WIDTH)
    qg = jnp.tile(gqa_q_norm_g, (1, 2)).reshape(DEPTH, 1, LANES)
    kg = jnp.tile(gqa_k_norm_g, (1, 2)).reshape(DEPTH, 1, LANES)
    dg = jnp.tile(diff_norm_g, (1, 2)).reshape(DEPTH, 1, LANES)
    n1 = norm1_g.reshape(DEPTH, 1, D_MODEL)
    n2 = norm2_g.reshape(DEPTH, 1, D_MODEL)

    for l in range(DEPTH):
        lam_init = 0.8 - 0.6 * math.exp(-0.3 * l)
        for name in ('p', 's'):
            g = groups[name]
            seq, tm, mod_row = g['seq'], g['tm'], g['mod_row']
            x = xs[name]
            lru, qx, k, v, dk, dv = _norm_proj(x, mod, n1, w_in_b, qg, kg, ones_blk, g['tables'],
                                               layer=l, seq=seq, tm=tm, mod_row=mod_row)
            h0, state_layer = (zero_state, 0) if name == 'p' else (state_lru, l)
            lru_o, st = _lru(lru, lru_conv_w, lru_cb, gw, gb, lru_lambda, h0,
                             layer=l, state_layer=state_layer, seq=seq, tr=256)
            att_o = _attention(qx, k, v, dk, dv, g['caches'], diff_lambda, dg,
                               layer=l, seq=seq, tq=g['tq'], lam_init=lam_init)
            xs[name] = _mix_ffn(lru_o, att_o, x, w_out_b, n2, up_b, ffn_conv_w, ffn_cb, wd_b, mod, fin,
                                layer=l, seq=seq, tm=tm, mod_row=mod_row, final=(l == DEPTH - 1))
            if name == 'p':
                new_k.append(k)
                new_v.append(v)
                new_dk.append(dk)
                new_dv.append(dv)
                new_st.append(st)

    y_prompt = xs['p'].reshape(bp, sp, D_MODEL)
    y_sample = xs['s'].reshape(bs, ss, D_MODEL)
    stack = lambda parts, shape: jnp.stack([p.reshape((bp,) + shape) for p in parts], axis=1)
    return (y_prompt, y_sample,
            stack(new_k, (sp, GQA_KV_HEADS, HEAD_DIM)),
            stack(new_v, (sp, GQA_KV_HEADS, HEAD_DIM)),
            stack(new_dk, (sp, DIFF_HEADS, 2, DIFF_QK_DIM)),
            stack(new_dv, (sp, DIFF_HEADS, HEAD_DIM)),
            jnp.stack(new_st, axis=1))
```

```python
import functools
import math

import numpy as np
import jax
import jax.numpy as jnp
from jax import lax
from jax.experimental import pallas as pl
from jax.experimental.pallas import tpu as pltpu

F32 = jnp.float32
BF16 = jnp.bfloat16

D_MODEL = 1024
DEPTH = 2
GRID_W = 64
HEAD_DIM = 64
LRU_WIDTH = 256
LRU_BLOCKS = 4
LRU_BLOCK_W = LRU_WIDTH // LRU_BLOCKS
LRU_C = 8.0
GQA_HEADS = 8
GQA_KV_HEADS = 2
DIFF_HEADS = 4
DIFF_QK_DIM = 32
D_FF = 2816
ROPE_THETA = 10000.0
EPS = 1e-6

LOG2_E = math.log2(math.e)
N_JOBS = GQA_HEADS + 2 * DIFF_HEADS
KEY_CHUNK = 256
ATTN_WIDTH = 512
LANES = 128
FFN_CHUNK = 256
N_FFN_CHUNKS = D_FF // FFN_CHUNK
COND_ROWS = 8
VMEM_LIMIT = 56 * 2 ** 20


def _params(*sem):
    return pltpu.CompilerParams(dimension_semantics=sem, vmem_limit_bytes=VMEM_LIMIT)


def _rms(x):
    return x * lax.rsqrt(jnp.mean(x * x, axis=-1, keepdims=True) + EPS)


def _ada_kernel(cond_ref, w_ref, b_ref, o_ref):
    c = cond_ref[...]
    s = c * jax.nn.sigmoid(c)
    o_ref[...] = jnp.dot(s.astype(BF16), w_ref[...].astype(BF16), preferred_element_type=F32) + b_ref[...]


def _ada(cond, ada_w, ada_b):
    tn = 1536
    width = 6 * D_MODEL
    return pl.pallas_call(
        _ada_kernel,
        grid=(DEPTH, width // tn),
        in_specs=[pl.BlockSpec((COND_ROWS, D_MODEL), lambda l, j: (0, 0)),
                  pl.BlockSpec((None, D_MODEL, tn), lambda l, j: (l, 0, j)),
                  pl.BlockSpec((None, 1, tn), lambda l, j: (l, 0, j))],
        out_specs=pl.BlockSpec((None, COND_ROWS, tn), lambda l, j: (l, 0, j)),
        out_shape=jax.ShapeDtypeStruct((DEPTH, COND_ROWS, width), F32),
        compiler_params=_params("arbitrary", "arbitrary"),
        name="ada",
    )(cond, ada_w, ada_b.reshape(DEPTH, 1, width))


def _group_mean_sq(x, ones_blk):
    sq = x * x
    hi = sq.astype(BF16)
    lo = (sq - hi.astype(F32)).astype(BF16)
    s = jnp.dot(hi, ones_blk, preferred_element_type=F32) + jnp.dot(lo, ones_blk, preferred_element_type=F32)
    return s * (1.0 / HEAD_DIM)


def _rope(x, cos, sin_neg, sin_pos, quarter):
    return x * cos + pltpu.roll(x, LANES - quarter, 1) * sin_neg + pltpu.roll(x, quarter, 1) * sin_pos


def _norm_proj_kernel(*refs, rope):
    if rope:
        (x_ref, mod_ref, n1_ref, w_ref, qg_ref, kg_ref, ones_ref, cq, snq, spq, cd, snd, spd,
         lru_ref, qx_ref, k_ref, v_ref, dk_ref, dv_ref) = refs
    else:
        (x_ref, mod_ref, n1_ref, w_ref, qg_ref, kg_ref, ones_ref,
         lru_ref, qx_ref, k_ref, v_ref, dk_ref, dv_ref) = refs
    mod = mod_ref[...]
    h = _rms(x_ref[...]) * (n1_ref[...] * (1.0 + mod[:, D_MODEL:2 * D_MODEL])) + mod[:, 0:D_MODEL]
    hb = h.astype(BF16)
    ones_blk = ones_ref[...]
    lane = lax.broadcasted_iota(jnp.int32, (1, LANES), 1)

    proj = jnp.dot(hb, w_ref[:, 512:1024], preferred_element_type=F32)
    for c in range(4):
        xc = proj[:, LANES * c:LANES * (c + 1)]
        xc = xc * lax.rsqrt(_group_mean_sq(xc, ones_blk) + EPS) * qg_ref[...]
        if rope:
            xc = _rope(xc, cq[...], snq[...], spq[...], HEAD_DIM // 4)
        xc = xc * (HEAD_DIM ** -0.5 * LOG2_E)
        xr = pltpu.roll(xc, HEAD_DIM, 1)
        for par in range(2):
            j = 2 * c + par
            want = j // (GQA_HEADS // GQA_KV_HEADS)
            src = xc if par == want else xr
            qx_ref[j] = jnp.where(lane // HEAD_DIM == want, src, 0.0).astype(BF16)

    proj = jnp.dot(hb, w_ref[:, 1024:2048], preferred_element_type=F32)
    kc = proj[:, 0:128]
    kc = kc * lax.rsqrt(_group_mean_sq(kc, ones_blk) + EPS) * kg_ref[...]
    if rope:
        kc = _rope(kc, cq[...], snq[...], spq[...], HEAD_DIM // 4)
    k_ref[...] = kc
    v_ref[...] = proj[:, 128:256]

    for c in range(2):
        xc = proj[:, 256 + LANES * c:256 + LANES * (c + 1)]
        if rope:
            xc = _rope(xc, cd[...], snd[...], spd[...], DIFF_QK_DIM // 4)
        xc = xc * (DIFF_QK_DIM ** -0.5 * LOG2_E)
        for g in range(4):
            qx_ref[GQA_HEADS + 4 * c + g] = jnp.where(lane // DIFF_QK_DIM == g, xc, 0.0).astype(BF16)
        kc = proj[:, 512 + LANES * c:512 + LANES * (c + 1)]
        if rope:
            kc = _rope(kc, cd[...], snd[...], spd[...], DIFF_QK_DIM // 4)
        dk_ref[:, LANES * c:LANES * (c + 1)] = kc
    dv_ref[...] = proj[:, 768:1024]

    lru_ref[...] = jnp.dot(hb, w_ref[:, 0:512], preferred_element_type=F32)


def _norm_proj(x, mod, n1, w_in, qg, kg, ones_blk, tables, *, layer, seq, tm, mod_row):
    n = x.shape[0]
    rope = tables is not None
    nts = seq // tm
    row = lambda i: (i, 0)
    const = lambda i: (0, 0)
    per_layer = lambda i: (layer, 0, 0)
    in_specs = [pl.BlockSpec((tm, D_MODEL), row),
                pl.BlockSpec((None, 1, 6 * D_MODEL), lambda i: (layer * COND_ROWS + mod_row(i * tm // seq), 0, 0)),
                pl.BlockSpec((None, 1, D_MODEL), per_layer),
                pl.BlockSpec((None, D_MODEL, 2048), per_layer),
                pl.BlockSpec((None, 1, LANES), per_layer),
                pl.BlockSpec((None, 1, LANES), per_layer),
                pl.BlockSpec((LANES, LANES), const)]
    args = [x, mod, n1, w_in, qg, kg, ones_blk]
    if rope:
        in_specs += [pl.BlockSpec((tm, LANES), lambda i: (i % nts, 0))] * 6
        args += list(tables)
    slab = lambda i: (0, i, 0)
    out_specs = [pl.BlockSpec((tm, 512), row),
                 pl.BlockSpec((N_JOBS, tm, LANES), slab),
                 pl.BlockSpec((tm, 128), row),
                 pl.BlockSpec((tm, 128), row),
                 pl.BlockSpec((tm, 256), row),
                 pl.BlockSpec((tm, 256), row)]
    out_shape = [jax.ShapeDtypeStruct((n, 512), F32),
                 jax.ShapeDtypeStruct((N_JOBS, n, LANES), BF16),
                 jax.ShapeDtypeStruct((n, 128), F32),
                 jax.ShapeDtypeStruct((n, 128), F32),
                 jax.ShapeDtypeStruct((n, 256), F32),
                 jax.ShapeDtypeStruct((n, 256), F32)]
    return pl.pallas_call(
        functools.partial(_norm_proj_kernel, rope=rope),
        grid=(n // tm,), in_specs=in_specs, out_specs=out_specs, out_shape=out_shape,
        compiler_params=_params("arbitrary"), name="norm_proj",
    )(*args)


def _chunk_scan(a, u, row, reverse):
    for d in (1, 2, 4):
        shift = 8 - d if reverse else d
        a_s = pltpu.roll(a, shift, 0)
        u_s = pltpu.roll(u, shift, 0)
        m = (row < 8 - d) if reverse else (row >= d)
        u = jnp.where(m, a * u_s + u, u)
        a = jnp.where(m, a * a_s, a)
    return a, u


def _lru_kernel(x_ref, xp_ref, xn_ref, cw_ref, cb_ref, gw_ref, gb_ref, lam_ref, h0_ref, out_ref, st_ref,
                af, uf, ab, ub, gg, *, seq, tr):
    j = pl.program_id(1)
    nt = seq // tr
    xg = x_ref[...]
    x = xg[:, 0:LRU_WIDTH]
    prev = jnp.where(j > 0, xp_ref[...], 0.0)
    nxt = jnp.where(j < nt - 1, xn_ref[...], 0.0)
    xe = jnp.concatenate([prev, x, nxt], axis=0)
    ne = tr + 16
    cw = cw_ref[...]
    xc = (cw[0:1] * pltpu.roll(xe, 2, 0)[8:8 + tr] + cw[1:2] * pltpu.roll(xe, 1, 0)[8:8 + tr]
          + cw[2:3] * x + cw[3:4] * pltpu.roll(xe, ne - 1, 0)[8:8 + tr] + cb_ref[...])
    sg = 0.5 * jnp.tanh(jnp.dot(xc.astype(BF16), gw_ref[...], preferred_element_type=F32) + gb_ref[...]) + 0.5
    z = -lam_ref[...]
    neg_c_softplus = -LRU_C * (jnp.maximum(z, 0.0) + jnp.log1p(jnp.exp(-jnp.abs(z))))
    r0 = pl.multiple_of(j * tr, tr)
    for d, (a_s, u_s) in enumerate(((af, uf), (ab, ub))):
        r = sg[:, 512 * d:512 * d + LRU_WIDTH]
        i = sg[:, 512 * d + LRU_WIDTH:512 * (d + 1)]
        log_a = neg_c_softplus[d:d + 1] * r
        a = jnp.exp(log_a)
        a_s[pl.ds(r0, tr), :] = a
        u_s[pl.ds(r0, tr), :] = jnp.sqrt(-jnp.tanh(log_a) * (a * a + 1.0)) * (i * xc)
    gg[pl.ds(r0, tr), :] = jax.nn.gelu(xg[:, LRU_WIDTH:2 * LRU_WIDTH])

    @pl.when(j == nt - 1)
    def _():
        row = lax.broadcasted_iota(jnp.int32, (8, LRU_WIDTH), 0)
        nchunk = seq // 8

        def body(c, carry):
            hf, hb = carry
            rf = pl.multiple_of(c * 8, 8)
            a, u = _chunk_scan(af[pl.ds(rf, 8), :], uf[pl.ds(rf, 8), :], row, False)
            hs = u + a * hf
            uf[pl.ds(rf, 8), :] = hs
            rb = pl.multiple_of((nchunk - 1 - c) * 8, 8)
            a2, u2 = _chunk_scan(ab[pl.ds(rb, 8), :], ub[pl.ds(rb, 8), :], row, True)
            hs2 = u2 + a2 * hb
            ub[pl.ds(rb, 8), :] = hs2
            return hs[7:8, :], hs2[0:1, :]

        h0 = h0_ref[...]
        hf, hb = lax.fori_loop(0, nchunk, body, (h0[0:1], h0[1:2]), unroll=4)
        st_ref[...] = jnp.concatenate([hf, hb], axis=0)

        def obody(t, carry):
            r = pl.multiple_of(t * tr, tr)
            out_ref[pl.ds(r, tr), :] = (gg[pl.ds(r, tr), :] * (uf[pl.ds(r, tr), :] + ub[pl.ds(r, tr), :])).astype(BF16)
            return carry

        lax.fori_loop(0, nt, obody, 0)


def _lru(lru, cw, cb, gw, gb, lam, h0, *, layer, state_layer, seq, tr):
    n = lru.shape[0]
    b = n // seq
    nt = seq // tr
    nb8 = n // 8
    per_layer = lambda bi, j: (layer, 0, 0)
    in_specs = [pl.BlockSpec((tr, 512), lambda bi, j: (bi * nt + j, 0)),
                pl.BlockSpec((8, LRU_WIDTH), lambda bi, j: (jnp.maximum((bi * nt + j) * (tr // 8) - 1, 0), 0)),
                pl.BlockSpec((8, LRU_WIDTH), lambda bi, j: (jnp.minimum((bi * nt + j + 1) * (tr // 8), nb8 - 1), 0)),
                pl.BlockSpec((None, 4, LRU_WIDTH), per_layer),
                pl.BlockSpec((None, 1, LRU_WIDTH), per_layer),
                pl.BlockSpec((None, LRU_WIDTH, 4 * LRU_WIDTH), per_layer),
                pl.BlockSpec((None, 1, 4 * LRU_WIDTH), per_layer),
                pl.BlockSpec((None, 2, LRU_WIDTH), per_layer),
                pl.BlockSpec((None, None, 2, LRU_WIDTH), lambda bi, j: (bi, state_layer, 0, 0))]
    out_specs = [pl.BlockSpec((seq, LRU_WIDTH), lambda bi, j: (bi, 0)),
                 pl.BlockSpec((None, 2, LRU_WIDTH), lambda bi, j: (bi, 0, 0))]
    out_shape = [jax.ShapeDtypeStruct((n, LRU_WIDTH), BF16),
                 jax.ShapeDtypeStruct((b, 2, LRU_WIDTH), F32)]
    return pl.pallas_call(
        functools.partial(_lru_kernel, seq=seq, tr=tr),
        grid=(b, nt), in_specs=in_specs, out_specs=out_specs, out_shape=out_shape,
        scratch_shapes=[pltpu.VMEM((seq, LRU_WIDTH), F32)] * 5,
        compiler_params=_params("arbitrary", "arbitrary"), name="lru",
    )(lru, lru, lru, cw, cb, gw, gb, lam, h0)


_NT = (((1,), (1,)), ((), ()))


def _attn_kernel(*refs, n_ctx, seq, tq, pack, lam_init):
    if n_ctx:
        (q_ref, k_ref, v_ref, dk_ref, dv_ref, ck_ref, cv_ref, cdk_ref, cdv_ref, dl_ref, dg_ref,
         out_ref, ks, vts, s_buf0, s_buf1, e_buf0, e_buf1, o_buf) = refs
    else:
        (q_ref, k_ref, v_ref, dk_ref, dv_ref, dl_ref, dg_ref,
         out_ref, ks, vts, s_buf0, s_buf1, e_buf0, e_buf1, o_buf) = refs
    s_bufs = (s_buf0, s_buf1)
    e_bufs = (e_buf0, e_buf1)
    t_all = n_ctx + seq
    n_chunks = t_all // KEY_CHUNK

    def put(slab, new, ctx):
        if n_ctx:
            ks[slab, 0:n_ctx, :] = ctx[0].astype(BF16)
            vts[slab, :, 0:n_ctx] = ctx[1].T.astype(BF16)
        ks[slab, n_ctx:t_all, :] = new[0].astype(BF16)
        vts[slab, :, n_ctx:t_all] = new[1].T.astype(BF16)

    put(0, (k_ref[...], v_ref[...]), (ck_ref[...], cv_ref[...]) if n_ctx else None)
    for c in range(2):
        cols = slice(LANES * c, LANES * (c + 1))
        put(1 + c, (dk_ref[:, cols], dv_ref[:, cols]), (cdk_ref[:, cols], cdv_ref[:, cols]) if n_ctx else None)

    width = pack * tq
    upb = N_JOBS // pack
    n_units = upb * (seq // tq)

    def slab_of(t):
        g = (t % upb) * pack
        return jnp.where(g < GQA_HEADS, 0, 1 + (g - GQA_HEADS) // 4)

    def q_unit(t):
        u, qb = t % upb, t // upb
        r0 = qb * tq if isinstance(t, int) else pl.multiple_of(qb * tq, tq)
        if pack == 1:
            return q_ref[u, pl.ds(r0, tq), :]
        return q_ref[pl.ds(u * pack, pack), pl.ds(r0, tq), :].reshape(width, LANES)

    def fold8(x, op):
        acc = x[0:8]
        for r in range(1, KEY_CHUNK // 8):
            acc = op(acc, x[8 * r:8 * (r + 1)])
        return acc

    def run(t, par, m_prev, l_prev, do_a=True, do_b=True, do_c=True):
        ga, gb, gc = t, t - 2, t - 4
        s_buf, e_buf = s_bufs[par], e_bufs[par]
        m_acc = l_acc = None
        if do_a:
            qa = q_unit(ga)
            ka = slab_of(ga)
            m_acc = jnp.full((8, width), -jnp.inf, F32)
        if do_b:
            m_row = jnp.max(m_prev, axis=0, keepdims=True)
            l_acc = jnp.zeros((8, width), F32)
        if do_c:
            vc = slab_of(gc)
            o_acc = jnp.zeros((LANES, width), F32)
        for c in range(n_chunks):
            rows = slice(KEY_CHUNK * c, KEY_CHUNK * (c + 1))
            if do_c:
                o_acc = o_acc + jnp.dot(vts[vc, :, rows], e_buf[rows, :], preferred_element_type=F32)
            if do_b:
                e = jnp.exp2(s_buf[rows, :] - m_row)
                e_buf[rows, :] = e.astype(BF16)
                l_acc = l_acc + fold8(e, jnp.add)
            if do_a:
                s = lax.dot_general(ks[ka, rows, :], qa, _NT, preferred_element_type=F32)
                s_buf[rows, :] = s
                m_acc = jnp.maximum(m_acc, fold8(s, jnp.maximum))
        if do_c:
            o_buf[gc % upb] = o_acc * (1.0 / jnp.sum(l_prev, axis=0, keepdims=True))
        return m_acc, l_acc

    lane = lax.broadcasted_iota(jnp.int32, (1, LANES), 1)
    low_half = lane < HEAD_DIM

    def job_out(g, rows):
        return o_buf[g // pack, rows, tq * (g % pack):tq * (g % pack + 1)]

    def finish_block(qb):
        r0 = qb * tq if isinstance(qb, int) else pl.multiple_of(qb * tq, tq)
        out_rows = pl.ds(r0, tq)
        for c in range(4):
            h = c // 2
            rows = slice(HEAD_DIM * h, HEAD_DIM * (h + 1))
            ot = jnp.concatenate([job_out(2 * c, rows), job_out(2 * c + 1, rows)], axis=0)
            out_ref[out_rows, LANES * c:LANES * (c + 1)] = ot.T.astype(BF16)

        dl = dl_ref[...]
        lam = (jnp.exp(jnp.sum(dl[0:1] * dl[1:2], axis=-1, keepdims=True))
               - jnp.exp(jnp.sum(dl[2:3] * dl[3:4], axis=-1, keepdims=True)) + lam_init)

        for c in range(2):
            parts = []
            for par in range(2):
                g1 = GQA_HEADS + 2 * (2 * c + par)
                rows = slice(HEAD_DIM * par, HEAD_DIM * (par + 1))
                parts.append(job_out(g1, rows) - lam * job_out(g1 + 1, rows))
            o = jnp.concatenate(parts, axis=0).T
            sq = o * o
            ms = jnp.where(low_half,
                           jnp.sum(jnp.where(low_half, sq, 0.0), axis=-1, keepdims=True),
                           jnp.sum(jnp.where(low_half, 0.0, sq), axis=-1, keepdims=True)) * (1.0 / HEAD_DIM)
            out_ref[out_rows, GQA_HEADS * HEAD_DIM + LANES * c:GQA_HEADS * HEAD_DIM + LANES * (c + 1)] = (
                o * lax.rsqrt(ms + EPS) * dg_ref[...] * (1.0 - lam_init)).astype(BF16)

    bpb = upb // 2
    n_bodies = n_units // 2

    def body(i, carry, **stages):
        (m0, m1), (l0, l1) = carry
        m0, l0 = run(2 * i, 0, m0, l0, **stages)
        m1, l1 = run(2 * i + 1, 1, m1, l1, **stages)
        if isinstance(i, int):
            if i > 1 and (i - 1) % bpb == 0:
                finish_block((i - 1) // bpb - 1)
        elif n_bodies > bpb:
            @pl.when((i - 1) % bpb == 0)
            def _():
                finish_block((i - 1) // bpb - 1)
        return (m0, m1), (l0, l1)

    carry = body(0, ((None, None), (None, None)), do_b=False, do_c=False)
    carry = body(1, (carry[0], (None, None)), do_c=False)
    carry = lax.fori_loop(2, n_bodies, body, carry)
    carry = body(n_bodies, carry, do_a=False)
    body(n_bodies + 1, ((None, None), carry[1]), do_a=False, do_b=False)


def _attention(qx, k, v, dk, dv, caches, dl, dg, *, layer, seq, tq, lam_init):
    pack = ATTN_WIDTH // tq
    n = k.shape[0]
    b = n // seq
    n_ctx = 0 if caches is None else caches[0].shape[2]
    t_all = n_ctx + seq
    per_b = lambda bi: (bi, 0)
    once = pl.Buffered(1) if seq > ATTN_WIDTH else None
    in_specs = [pl.BlockSpec((N_JOBS, seq, LANES), lambda bi: (0, bi, 0)),
                pl.BlockSpec((seq, 128), per_b, pipeline_mode=once),
                pl.BlockSpec((seq, 128), per_b, pipeline_mode=once),
                pl.BlockSpec((seq, 256), per_b, pipeline_mode=once),
                pl.BlockSpec((seq, 256), per_b, pipeline_mode=once)]
    args = [qx, k, v, dk, dv]
    if n_ctx:
        cache_idx = lambda bi: (bi, layer, 0, 0)
        in_specs += [pl.BlockSpec((None, None, n_ctx, 128), cache_idx),
                     pl.BlockSpec((None, None, n_ctx, 128), cache_idx),
                     pl.BlockSpec((None, None, n_ctx, 256), cache_idx),
                     pl.BlockSpec((None, None, n_ctx, 256), cache_idx)]
        args += list(caches)
    in_specs += [pl.BlockSpec((None, 4, DIFF_QK_DIM), lambda bi: (layer, 0, 0)),
                 pl.BlockSpec((None, 1, LANES), lambda bi: (layer, 0, 0))]
    args += [dl, dg]
    return pl.pallas_call(
        functools.partial(_attn_kernel, n_ctx=n_ctx, seq=seq, tq=tq, pack=pack, lam_init=lam_init),
        grid=(b,), in_specs=in_specs,
        out_specs=pl.BlockSpec((seq, D_MODEL - LRU_WIDTH), per_b),
        out_shape=jax.ShapeDtypeStruct((n, D_MODEL - LRU_WIDTH), BF16),
        scratch_shapes=[pltpu.VMEM((3, t_all, LANES), BF16), pltpu.VMEM((3, LANES, t_all), BF16),
                        pltpu.VMEM((t_all, ATTN_WIDTH), F32), pltpu.VMEM((t_all, ATTN_WIDTH), F32),
                        pltpu.VMEM((t_all, ATTN_WIDTH), BF16), pltpu.VMEM((t_all, ATTN_WIDTH), BF16),
                        pltpu.VMEM((N_JOBS // pack, LANES, ATTN_WIDTH), F32)],
        compiler_params=_params("arbitrary"), name="attention",
    )(*args)


HALO = 16


def _mix_ffn_kernel(lru_ref, lrup_ref, lrun_ref, att_ref, attp_ref, attn_ref, x_ref, xp_ref, xn_ref,
                    wo_ref, n2_ref, up_ref, cw_ref, cb_ref, wd_ref, mod_ref, fin_ref, o_ref,
                    mix, xe, he, act, *, seq, tm, final):
    i = pl.program_id(0)
    nts = seq // tm
    pos = i % nts
    ne = tm + 2 * HALO
    for r0, nr, lru_r, att_r, x_r in ((0, HALO, lrup_ref, attp_ref, xp_ref),
                                      (HALO, tm, lru_ref, att_ref, x_ref),
                                      (HALO + tm, HALO, lrun_ref, attn_ref, xn_ref)):
        mix[r0:r0 + nr, 0:LRU_WIDTH] = lru_r[...]
        mix[r0:r0 + nr, LRU_WIDTH:D_MODEL] = att_r[...]
        xe[r0:r0 + nr, :] = x_r[...]
    mod = mod_ref[...]
    x1 = xe[...] + mod[:, 2 * D_MODEL:3 * D_MODEL] * jnp.dot(mix[...], wo_ref[...], preferred_element_type=F32)
    xe[...] = x1
    he[...] = (_rms(x1) * (n2_ref[...] * (1.0 + mod[:, 4 * D_MODEL:5 * D_MODEL]))
               + mod[:, 3 * D_MODEL:4 * D_MODEL]).astype(BF16)

    @pl.when(pos == 0)
    def _():
        he[0:HALO, :] = jnp.zeros((HALO, D_MODEL), BF16)

    @pl.when(pos == nts - 1)
    def _():
        he[HALO + tm:ne, :] = jnp.zeros((HALO, D_MODEL), BF16)

    def up(c, half):
        cols = slice(half * D_FF + FFN_CHUNK * c, half * D_FF + FFN_CHUNK * (c + 1))
        u = jnp.dot(he[...], up_ref[:, cols], preferred_element_type=F32)
        cw = cw_ref[:, cols]
        return (cw[0:1] * pltpu.roll(u, 1, 0)[HALO:HALO + tm] + cw[1:2] * u[HALO:HALO + tm]
                + cw[2:3] * pltpu.roll(u, ne - 1, 0)[HALO:HALO + tm] + cb_ref[:, cols])

    for c in range(N_FFN_CHUNKS):
        a, g = up(c, 0), up(c, 1)
        act[:, FFN_CHUNK * c:FFN_CHUNK * (c + 1)] = (g * jax.nn.sigmoid(g) * a).astype(BF16)
    down = jnp.dot(act[...], wd_ref[...], preferred_element_type=F32)
    x2 = xe[HALO:HALO + tm, :] + mod[:, 5 * D_MODEL:6 * D_MODEL] * down
    if final:
        x2 = _rms(x2) * fin_ref[...]
    o_ref[...] = x2


def _mix_ffn(lru_o, att_o, x, w_out, n2, up, cw, cb, wd, mod, fin_g, *, layer, seq, tm, mod_row, final):
    n = x.shape[0]
    nts = seq // tm
    nbh = n // HALO
    row = lambda i: (i, 0)
    prev = lambda i: (jnp.maximum(i * (tm // HALO) - 1, 0), 0)
    nxt = lambda i: (jnp.minimum((i + 1) * (tm // HALO), nbh - 1), 0)
    c2 = lambda i: (0, 0)
    per_layer = lambda i: (layer, 0, 0)
    att_w = D_MODEL - LRU_WIDTH

    def with_halo(width):
        return [pl.BlockSpec((tm, width), row), pl.BlockSpec((HALO, width), prev), pl.BlockSpec((HALO, width), nxt)]

    in_specs = (with_halo(LRU_WIDTH) + with_halo(att_w) + with_halo(D_MODEL)
                + [pl.BlockSpec((None, D_MODEL, D_MODEL), per_layer),
                   pl.BlockSpec((None, 1, D_MODEL), per_layer),
                   pl.BlockSpec((None, D_MODEL, 2 * D_FF), per_layer),
                   pl.BlockSpec((None, 3, 2 * D_FF), per_layer),
                   pl.BlockSpec((None, 1, 2 * D_FF), per_layer),
                   pl.BlockSpec((None, D_FF, D_MODEL), per_layer),
                   pl.BlockSpec((None, 1, 6 * D_MODEL), lambda i: (layer * COND_ROWS + mod_row(i // nts), 0, 0)),
                   pl.BlockSpec((1, D_MODEL), c2)])
    return pl.pallas_call(
        functools.partial(_mix_ffn_kernel, seq=seq, tm=tm, final=final),
        grid=(n // tm,), in_specs=in_specs,
        out_specs=pl.BlockSpec((tm, D_MODEL), row),
        out_shape=jax.ShapeDtypeStruct((n, D_MODEL), F32),
        scratch_shapes=[pltpu.VMEM((tm + 2 * HALO, D_MODEL), BF16), pltpu.VMEM((tm + 2 * HALO, D_MODEL), F32),
                        pltpu.VMEM((tm + 2 * HALO, D_MODEL), BF16), pltpu.VMEM((tm, D_FF), BF16)],
        compiler_params=_params("arbitrary"), name="mix_ffn",
    )(lru_o, lru_o, lru_o, att_o, att_o, att_o, x, x, x, w_out, n2, up, cw, cb, wd, mod, fin_g)


def _rope_tables(seq, dim):
    rows = seq // GRID_W
    t_row = np.repeat(np.arange(rows, dtype=np.float64), GRID_W)
    t_col = np.tile(np.arange(GRID_W, dtype=np.float64), rows)
    axis_dim = dim // 2
    inv = ROPE_THETA ** (-np.arange(0, axis_dim, 2, dtype=np.float64) / axis_dim)
    ar = t_row[:, None] * inv
    ac = t_col[:, None] * inv
    ang = np.concatenate([ar, ar, ac, ac], axis=-1)
    reps = LANES // dim
    cos = np.tile(np.cos(ang), (1, reps))
    sin = np.tile(np.sin(ang), (1, reps))
    first = (np.arange(LANES) % (dim // 2)) < (dim // 4)
    sin_neg = np.where(first, -sin, 0.0)
    sin_pos = np.where(first, 0.0, sin)
    return tuple(jnp.asarray(t, F32) for t in (cos, sin_neg, sin_pos))


def _gate_weights(w, b):
    nl = w.shape[0]
    eye = jnp.eye(LRU_BLOCKS, dtype=w.dtype)
    wp = w.reshape(nl, 2, LRU_BLOCKS, LRU_BLOCK_W, 2, LRU_BLOCK_W)
    dense = jnp.einsum('lpndqe,nm->lndpqme', wp, eye).reshape(nl, LRU_WIDTH, 4 * LRU_WIDTH)
    bias = jnp.transpose(b.reshape(nl, 2, LRU_BLOCKS, 2, LRU_BLOCK_W), (0, 1, 3, 2, 4)).reshape(nl, 1, 4 * LRU_WIDTH)
    return (0.5 * dense).astype(BF16), 0.5 * bias


def kernel(x_prompt, x_sample, cache_gqa_k, cache_gqa_v, cache_diff_k, cache_diff_v, state_lru, c, c_ctx,
           norm1_g, norm2_g, final_norm_g, ada_w, ada_b, w_in, w_out, lru_conv_w, lru_conv_b, lru_gate_w,
           lru_gate_b, lru_lambda, gqa_q_norm_g, gqa_k_norm_g, diff_lambda, diff_norm_g, ffn_w_up, ffn_conv_w,
           ffn_conv_b, ffn_w_down):
    bp, sp, _ = x_prompt.shape
    bs, ss, _ = x_sample.shape
    n_ctx = cache_gqa_k.shape[2]

    cond = jnp.concatenate([c_ctx[None, :], c, jnp.zeros((COND_ROWS - 1 - bs, D_MODEL), F32)], axis=0)
    mod_all = _ada(cond, ada_w, ada_b)

    caches = (cache_gqa_k.reshape(bs, DEPTH, n_ctx, 128), cache_gqa_v.reshape(bs, DEPTH, n_ctx, 128),
              cache_diff_k.reshape(bs, DEPTH, n_ctx, 256), cache_diff_v.reshape(bs, DEPTH, n_ctx, 256))
    tables = _rope_tables(ss, HEAD_DIM) + _rope_tables(ss, DIFF_QK_DIM)
    ones_blk = jnp.asarray(np.kron(np.eye(2), np.ones((HEAD_DIM, HEAD_DIM))), BF16)
    zero_state = jnp.zeros((bp, 1, 2, LRU_WIDTH), F32)

    groups = {
        'p': dict(seq=sp, tm=256, tq=256, mod_row=lambda b: 0, tables=None, caches=None),
        's': dict(seq=ss, tm=512, tq=512, mod_row=lambda b: 1 + b, tables=tables, caches=caches),
    }
    xs = {'p': x_prompt.reshape(bp * sp, D_MODEL), 's': x_sample.reshape(bs * ss, D_MODEL)}
    new_k, new_v, new_dk, new_dv, new_st = [], [], [], [], []

    mod = mod_all.reshape(DEPTH * COND_ROWS, 1, 6 * D_MODEL)
    w_in_b = w_in.astype(BF16)
    w_out_b = w_out.astype(BF16)
    up_b = ffn_w_up.astype(BF16)
    wd_b = ffn_w_down.astype(BF16)
    ffn_cb = ffn_conv_b.reshape(DEPTH, 1, 2 * D_FF)
    fin = final_norm_g[None, :]
    gw, gb = _gate_weights(lru_gate_w, lru_gate_b)
    lru_cb = lru_conv_b.reshape(DEPTH, 1, LRU_WIDTH)
    qg = jnp.tile(gqa_q_norm_g, (1, 2)).reshape(DEPTH, 1, LANES)
    kg = jnp.tile(gqa_k_norm_g, (1, 2)).reshape(DEPTH, 1, LANES)
    dg = jnp.tile(diff_norm_g, (1, 2)).reshape(DEPTH, 1, LANES)
    n1 = norm1_g.reshape(DEPTH, 1, D_MODEL)
    n2 = norm2_g.reshape(DEPTH, 1, D_MODEL)

    for l in range(DEPTH):
        lam_init = 0.8 - 0.6 * math.exp(-0.3 * l)
        for name in ('p', 's'):
            g = groups[name]
            seq, tm, mod_row = g['seq'], g['tm'], g['mod_row']
            x = xs[name]
            lru, qx, k, v, dk, dv = _norm_proj(x, mod, n1, w_in_b, qg, kg, ones_blk, g['tables'],
                                               layer=l, seq=seq, tm=512, mod_row=mod_row)
            h0, state_layer = (zero_state, 0) if name == 'p' else (state_lru, l)
            lru_o, st = _lru(lru, lru_conv_w, lru_cb, gw, gb, lru_lambda, h0,
                             layer=l, state_layer=state_layer, seq=seq, tr=256)
            att_o = _attention(qx, k, v, dk, dv, g['caches'], diff_lambda, dg,
                               layer=l, seq=seq, tq=g['tq'], lam_init=lam_init)
            xs[name] = _mix_ffn(lru_o, att_o, x, w_out_b, n2, up_b, ffn_conv_w, ffn_cb, wd_b, mod, fin,
                                layer=l, seq=seq, tm=tm, mod_row=mod_row, final=(l == DEPTH - 1))
            if name == 'p':
                new_k.append(k)
                new_v.append(v)
                new_dk.append(dk)
                new_dv.append(dv)
                new_st.append(st)

    y_prompt = xs['p'].reshape(bp, sp, D_MODEL)
    y_sample = xs['s'].reshape(bs, ss, D_MODEL)
    stack = lambda parts, shape: jnp.stack([p.reshape((bp,) + shape) for p in parts], axis=1)
    return (y_prompt, y_sample,
            stack(new_k, (sp, GQA_KV_HEADS, HEAD_DIM)),
            stack(new_v, (sp, GQA_KV_HEADS, HEAD_DIM)),
            stack(new_dk, (sp, DIFF_HEADS, 2, DIFF_QK_DIM)),
            stack(new_dv, (sp, DIFF_HEADS, HEAD_DIM)),
            jnp.stack(new_st, axis=1))
```

```python
import functools
import math

import numpy as np
import jax
import jax.numpy as jnp
from jax import lax
from jax.experimental import pallas as pl
from jax.experimental.pallas import tpu as pltpu

F32 = jnp.float32
BF16 = jnp.bfloat16

D_MODEL = 1024
DEPTH = 2
GRID_W = 64
HEAD_DIM = 64
LRU_WIDTH = 256
LRU_BLOCKS = 4
LRU_BLOCK_W = LRU_WIDTH // LRU_BLOCKS
LRU_C = 8.0
GQA_HEADS = 8
GQA_KV_HEADS = 2
DIFF_HEADS = 4
DIFF_QK_DIM = 32
D_FF = 2816
ROPE_THETA = 10000.0
EPS = 1e-6

LRU_IN = 2 * LRU_WIDTH
GQA_WIDTH = GQA_HEADS * HEAD_DIM
KV_WIDTH = GQA_KV_HEADS * HEAD_DIM
DIFF_WIDTH = DIFF_HEADS * HEAD_DIM
Q_OFF = LRU_IN
KV_OFF = Q_OFF + GQA_WIDTH
IN_WIDTH = KV_OFF + 2 * KV_WIDTH + 3 * DIFF_WIDTH
LOG2_E = math.log2(math.e)
N_JOBS = GQA_HEADS + 2 * DIFF_HEADS
KEY_CHUNK = 256
ATTN_WIDTH = 512
LANES = 128
FFN_CHUNK = 256
N_FFN_CHUNKS = D_FF // FFN_CHUNK
COND_ROWS = 8
VMEM_LIMIT = 56 * 2 ** 20


def _params(*sem):
    return pltpu.CompilerParams(dimension_semantics=sem, vmem_limit_bytes=VMEM_LIMIT)


def _rms(x):
    return x * lax.rsqrt(jnp.mean(x * x, axis=-1, keepdims=True) + EPS)


def _ada_kernel(cond_ref, w_ref, b_ref, o_ref):
    c = cond_ref[...]
    s = c * jax.nn.sigmoid(c)
    o_ref[...] = jnp.dot(s.astype(BF16), w_ref[...].astype(BF16), preferred_element_type=F32) + b_ref[...]


def _ada(cond, ada_w, ada_b):
    tn = 1536
    width = 6 * D_MODEL
    return pl.pallas_call(
        _ada_kernel,
        grid=(DEPTH, width // tn),
        in_specs=[pl.BlockSpec((COND_ROWS, D_MODEL), lambda l, j: (0, 0)),
                  pl.BlockSpec((None, D_MODEL, tn), lambda l, j: (l, 0, j)),
                  pl.BlockSpec((None, 1, tn), lambda l, j: (l, 0, j))],
        out_specs=pl.BlockSpec((None, COND_ROWS, tn), lambda l, j: (l, 0, j)),
        out_shape=jax.ShapeDtypeStruct((DEPTH, COND_ROWS, width), F32),
        compiler_params=_params("arbitrary", "arbitrary"),
        name="ada",
    )(cond, ada_w, ada_b.reshape(DEPTH, 1, width))


def _group_mean_sq(x, ones_blk):
    sq = x * x
    hi = sq.astype(BF16)
    lo = (sq - hi.astype(F32)).astype(BF16)
    s = jnp.dot(hi, ones_blk, preferred_element_type=F32) + jnp.dot(lo, ones_blk, preferred_element_type=F32)
    return s * (1.0 / HEAD_DIM)


def _rope(x, cos, sin_neg, sin_pos, quarter):
    return x * cos + pltpu.roll(x, LANES - quarter, 1) * sin_neg + pltpu.roll(x, quarter, 1) * sin_pos


def _norm_proj_kernel(*refs, rope):
    if rope:
        (x_ref, mod_ref, n1_ref, w_ref, qg_ref, kg_ref, ones_ref, cq, snq, spq, cd, snd, spd,
         lru_ref, qx_ref, k_ref, v_ref, dk_ref, dv_ref) = refs
    else:
        (x_ref, mod_ref, n1_ref, w_ref, qg_ref, kg_ref, ones_ref,
         lru_ref, qx_ref, k_ref, v_ref, dk_ref, dv_ref) = refs
    mod = mod_ref[...]
    h = _rms(x_ref[...]) * (n1_ref[...] * (1.0 + mod[:, D_MODEL:2 * D_MODEL])) + mod[:, 0:D_MODEL]
    hb = h.astype(BF16)
    ones_blk = ones_ref[...]
    lane = lax.broadcasted_iota(jnp.int32, (1, LANES), 1)

    proj = jnp.dot(hb, w_ref[:, Q_OFF:KV_OFF], preferred_element_type=F32)
    for c in range(4):
        xc = proj[:, LANES * c:LANES * (c + 1)]
        xc = xc * lax.rsqrt(_group_mean_sq(xc, ones_blk) + EPS) * qg_ref[...]
        if rope:
            xc = _rope(xc, cq[...], snq[...], spq[...], HEAD_DIM // 4)
        xc = xc * (HEAD_DIM ** -0.5 * LOG2_E)
        xr = pltpu.roll(xc, HEAD_DIM, 1)
        for par in range(2):
            j = 2 * c + par
            want = j // (GQA_HEADS // GQA_KV_HEADS)
            src = xc if par == want else xr
            qx_ref[j] = jnp.where(lane // HEAD_DIM == want, src, 0.0).astype(BF16)

    proj = jnp.dot(hb, w_ref[:, KV_OFF:IN_WIDTH], preferred_element_type=F32)
    dq_off = 2 * KV_WIDTH
    dk_off = dq_off + DIFF_WIDTH
    dv_off = dk_off + DIFF_WIDTH
    kc = proj[:, 0:KV_WIDTH]
    kc = kc * lax.rsqrt(_group_mean_sq(kc, ones_blk) + EPS) * kg_ref[...]
    if rope:
        kc = _rope(kc, cq[...], snq[...], spq[...], HEAD_DIM // 4)
    k_ref[...] = kc
    v_ref[...] = proj[:, KV_WIDTH:2 * KV_WIDTH]

    for c in range(2):
        xc = proj[:, dq_off + LANES * c:dq_off + LANES * (c + 1)]
        if rope:
            xc = _rope(xc, cd[...], snd[...], spd[...], DIFF_QK_DIM // 4)
        xc = xc * (DIFF_QK_DIM ** -0.5 * LOG2_E)
        for g in range(4):
            qx_ref[GQA_HEADS + 4 * c + g] = jnp.where(lane // DIFF_QK_DIM == g, xc, 0.0).astype(BF16)
        kc = proj[:, dk_off + LANES * c:dk_off + LANES * (c + 1)]
        if rope:
            kc = _rope(kc, cd[...], snd[...], spd[...], DIFF_QK_DIM // 4)
        dk_ref[:, LANES * c:LANES * (c + 1)] = kc
    dv_ref[...] = proj[:, dv_off:dv_off + DIFF_WIDTH]

    lru_ref[...] = jnp.dot(hb, w_ref[:, 0:LRU_IN], preferred_element_type=F32)


def _norm_proj(x, mod, n1, w_in, qg, kg, ones_blk, tables, *, layer, seq, tm, mod_row):
    n = x.shape[0]
    rope = tables is not None
    nts = seq // tm
    row = lambda i: (i, 0)
    const = lambda i: (0, 0)
    per_layer = lambda i: (layer, 0, 0)
    in_specs = [pl.BlockSpec((tm, D_MODEL), row),
                pl.BlockSpec((None, 1, 6 * D_MODEL), lambda i: (layer * COND_ROWS + mod_row(i * tm // seq), 0, 0)),
                pl.BlockSpec((None, 1, D_MODEL), per_layer),
                pl.BlockSpec((None, D_MODEL, IN_WIDTH), per_layer),
                pl.BlockSpec((None, 1, LANES), per_layer),
                pl.BlockSpec((None, 1, LANES), per_layer),
                pl.BlockSpec((LANES, LANES), const)]
    args = [x, mod, n1, w_in, qg, kg, ones_blk]
    if rope:
        in_specs += [pl.BlockSpec((tm, LANES), lambda i: (i % nts, 0))] * 6
        args += list(tables)
    slab = lambda i: (0, i, 0)
    widths = (LRU_IN, KV_WIDTH, KV_WIDTH, DIFF_WIDTH, DIFF_WIDTH)
    out_specs = [pl.BlockSpec((tm, wd), row) for wd in widths]
    out_shape = [jax.ShapeDtypeStruct((n, wd), F32) for wd in widths]
    out_specs.insert(1, pl.BlockSpec((N_JOBS, tm, LANES), slab))
    out_shape.insert(1, jax.ShapeDtypeStruct((N_JOBS, n, LANES), BF16))
    return pl.pallas_call(
        functools.partial(_norm_proj_kernel, rope=rope),
        grid=(n // tm,), in_specs=in_specs, out_specs=out_specs, out_shape=out_shape,
        compiler_params=_params("arbitrary"), name="norm_proj",
    )(*args)


def _chunk_scan(a, u, row, reverse):
    for d in (1, 2, 4):
        shift = 8 - d if reverse else d
        a_s = pltpu.roll(a, shift, 0)
        u_s = pltpu.roll(u, shift, 0)
        m = (row < 8 - d) if reverse else (row >= d)
        u = jnp.where(m, a * u_s + u, u)
        a = jnp.where(m, a * a_s, a)
    return a, u


def _lru_kernel(x_ref, xp_ref, xn_ref, cw_ref, cb_ref, gw_ref, gb_ref, lam_ref, h0_ref, out_ref, st_ref,
                af, uf, ab, ub, gg, *, seq, tr):
    j = pl.program_id(1)
    nt = seq // tr
    xg = x_ref[...]
    x = xg[:, 0:LRU_WIDTH]
    prev = jnp.where(j > 0, xp_ref[...], 0.0)
    nxt = jnp.where(j < nt - 1, xn_ref[...], 0.0)
    xe = jnp.concatenate([prev, x, nxt], axis=0)
    ne = tr + 16
    cw = cw_ref[...]
    xc = (cw[0:1] * pltpu.roll(xe, 2, 0)[8:8 + tr] + cw[1:2] * pltpu.roll(xe, 1, 0)[8:8 + tr]
          + cw[2:3] * x + cw[3:4] * pltpu.roll(xe, ne - 1, 0)[8:8 + tr] + cb_ref[...])
    sg = 0.5 * jnp.tanh(jnp.dot(xc.astype(BF16), gw_ref[...], preferred_element_type=F32) + gb_ref[...]) + 0.5
    z = -lam_ref[...]
    neg_c_softplus = -LRU_C * (jnp.maximum(z, 0.0) + jnp.log1p(jnp.exp(-jnp.abs(z))))
    r0 = pl.multiple_of(j * tr, tr)
    for d, (a_s, u_s) in enumerate(((af, uf), (ab, ub))):
        r = sg[:, 2 * LRU_WIDTH * d:2 * LRU_WIDTH * d + LRU_WIDTH]
        i = sg[:, 2 * LRU_WIDTH * d + LRU_WIDTH:2 * LRU_WIDTH * (d + 1)]
        log_a = neg_c_softplus[d:d + 1] * r
        a = jnp.exp(log_a)
        a_s[pl.ds(r0, tr), :] = a
        u_s[pl.ds(r0, tr), :] = jnp.sqrt(-jnp.tanh(log_a) * (a * a + 1.0)) * (i * xc)
    gg[pl.ds(r0, tr), :] = jax.nn.gelu(xg[:, LRU_WIDTH:2 * LRU_WIDTH])

    @pl.when(j == nt - 1)
    def _():
        row = lax.broadcasted_iota(jnp.int32, (8, LRU_WIDTH), 0)
        nchunk = seq // 8

        def body(c, carry):
            hf, hb = carry
            rf = pl.multiple_of(c * 8, 8)
            a, u = _chunk_scan(af[pl.ds(rf, 8), :], uf[pl.ds(rf, 8), :], row, False)
            hs = u + a * hf
            uf[pl.ds(rf, 8), :] = hs
            rb = pl.multiple_of((nchunk - 1 - c) * 8, 8)
            a2, u2 = _chunk_scan(ab[pl.ds(rb, 8), :], ub[pl.ds(rb, 8), :], row, True)
            hs2 = u2 + a2 * hb
            ub[pl.ds(rb, 8), :] = hs2
            return hs[7:8, :], hs2[0:1, :]

        h0 = h0_ref[...]
        hf, hb = lax.fori_loop(0, nchunk, body, (h0[0:1], h0[1:2]), unroll=4)
        st_ref[...] = jnp.concatenate([hf, hb], axis=0)

        def obody(t, carry):
            r = pl.multiple_of(t * tr, tr)
            out_ref[pl.ds(r, tr), :] = (gg[pl.ds(r, tr), :] * (uf[pl.ds(r, tr), :] + ub[pl.ds(r, tr), :])).astype(BF16)
            return carry

        lax.fori_loop(0, nt, obody, 0)


def _lru(lru, cw, cb, gw, gb, lam, h0, *, layer, state_layer, seq, tr):
    n = lru.shape[0]
    b = n // seq
    nt = seq // tr
    nb8 = n // 8
    per_layer = lambda bi, j: (layer, 0, 0)
    in_specs = [pl.BlockSpec((tr, LRU_IN), lambda bi, j: (bi * nt + j, 0)),
                pl.BlockSpec((8, LRU_WIDTH), lambda bi, j: (jnp.maximum((bi * nt + j) * (tr // 8) - 1, 0), 0)),
                pl.BlockSpec((8, LRU_WIDTH), lambda bi, j: (jnp.minimum((bi * nt + j + 1) * (tr // 8), nb8 - 1), 0)),
                pl.BlockSpec((None, 4, LRU_WIDTH), per_layer),
                pl.BlockSpec((None, 1, LRU_WIDTH), per_layer),
                pl.BlockSpec((None, LRU_WIDTH, 4 * LRU_WIDTH), per_layer),
                pl.BlockSpec((None, 1, 4 * LRU_WIDTH), per_layer),
                pl.BlockSpec((None, 2, LRU_WIDTH), per_layer),
                pl.BlockSpec((None, None, 2, LRU_WIDTH), lambda bi, j: (bi, state_layer, 0, 0))]
    out_specs = [pl.BlockSpec((seq, LRU_WIDTH), lambda bi, j: (bi, 0)),
                 pl.BlockSpec((None, 2, LRU_WIDTH), lambda bi, j: (bi, 0, 0))]
    out_shape = [jax.ShapeDtypeStruct((n, LRU_WIDTH), BF16),
                 jax.ShapeDtypeStruct((b, 2, LRU_WIDTH), F32)]
    return pl.pallas_call(
        functools.partial(_lru_kernel, seq=seq, tr=tr),
        grid=(b, nt), in_specs=in_specs, out_specs=out_specs, out_shape=out_shape,
        scratch_shapes=[pltpu.VMEM((seq, LRU_WIDTH), F32)] * 5,
        compiler_params=_params("arbitrary", "arbitrary"), name="lru",
    )(lru, lru, lru, cw, cb, gw, gb, lam, h0)


_NT = (((1,), (1,)), ((), ()))


def _attn_kernel(*refs, n_ctx, seq, tq, pack, lam_init):
    if n_ctx:
        (q_ref, k_ref, v_ref, dk_ref, dv_ref, ck_ref, cv_ref, cdk_ref, cdv_ref, dl_ref, dg_ref,
         out_ref, ks, vts, s_buf0, s_buf1, e_buf0, e_buf1, o_buf) = refs
    else:
        (q_ref, k_ref, v_ref, dk_ref, dv_ref, dl_ref, dg_ref,
         out_ref, ks, vts, s_buf0, s_buf1, e_buf0, e_buf1, o_buf) = refs
    s_bufs = (s_buf0, s_buf1)
    e_bufs = (e_buf0, e_buf1)
    t_all = n_ctx + seq
    n_chunks = t_all // KEY_CHUNK

    def put(slab, new, ctx):
        if n_ctx:
            ks[slab, 0:n_ctx, :] = ctx[0].astype(BF16)
            vts[slab, :, 0:n_ctx] = ctx[1].T.astype(BF16)
        ks[slab, n_ctx:t_all, :] = new[0].astype(BF16)
        vts[slab, :, n_ctx:t_all] = new[1].T.astype(BF16)

    put(0, (k_ref[...], v_ref[...]), (ck_ref[...], cv_ref[...]) if n_ctx else None)
    for c in range(2):
        cols = slice(LANES * c, LANES * (c + 1))
        put(1 + c, (dk_ref[:, cols], dv_ref[:, cols]), (cdk_ref[:, cols], cdv_ref[:, cols]) if n_ctx else None)

    width = pack * tq
    upb = N_JOBS // pack
    n_units = upb * (seq // tq)

    def slab_of(t):
        g = (t % upb) * pack
        return jnp.where(g < GQA_HEADS, 0, 1 + (g - GQA_HEADS) // 4)

    def q_unit(t):
        u, qb = t % upb, t // upb
        r0 = qb * tq if isinstance(t, int) else pl.multiple_of(qb * tq, tq)
        if pack == 1:
            return q_ref[u, pl.ds(r0, tq), :]
        return q_ref[pl.ds(u * pack, pack), pl.ds(r0, tq), :].reshape(width, LANES)

    def fold8(x, op):
        acc = x[0:8]
        for r in range(1, KEY_CHUNK // 8):
            acc = op(acc, x[8 * r:8 * (r + 1)])
        return acc

    def run(t, par, m_prev, l_prev, do_a=True, do_b=True, do_c=True):
        ga, gb, gc = t, t - 2, t - 4
        s_buf, e_buf = s_bufs[par], e_bufs[par]
        m_acc = l_acc = None
        if do_a:
            qa = q_unit(ga)
            ka = slab_of(ga)
            m_acc = jnp.full((8, width), -jnp.inf, F32)
        if do_b:
            m_row = jnp.max(m_prev, axis=0, keepdims=True)
            l_acc = jnp.zeros((8, width), F32)
        if do_c:
            vc = slab_of(gc)
            o_acc = jnp.zeros((LANES, width), F32)
        for c in range(n_chunks):
            rows = slice(KEY_CHUNK * c, KEY_CHUNK * (c + 1))
            if do_c:
                o_acc = o_acc + jnp.dot(vts[vc, :, rows], e_buf[rows, :], preferred_element_type=F32)
            if do_b:
                e = jnp.exp2(s_buf[rows, :] - m_row)
                e_buf[rows, :] = e.astype(BF16)
                l_acc = l_acc + fold8(e, jnp.add)
            if do_a:
                s = lax.dot_general(ks[ka, rows, :], qa, _NT, preferred_element_type=F32)
                s_buf[rows, :] = s
                m_acc = jnp.maximum(m_acc, fold8(s, jnp.maximum))
        if do_c:
            o_buf[gc % upb] = o_acc * (1.0 / jnp.sum(l_prev, axis=0, keepdims=True))
        return m_acc, l_acc

    lane = lax.broadcasted_iota(jnp.int32, (1, LANES), 1)
    low_half = lane < HEAD_DIM

    def job_out(g, rows):
        return o_buf[g // pack, rows, tq * (g % pack):tq * (g % pack + 1)]

    def finish_block(qb):
        r0 = qb * tq if isinstance(qb, int) else pl.multiple_of(qb * tq, tq)
        out_rows = pl.ds(r0, tq)
        for c in range(4):
            h = c // 2
            rows = slice(HEAD_DIM * h, HEAD_DIM * (h + 1))
            ot = jnp.concatenate([job_out(2 * c, rows), job_out(2 * c + 1, rows)], axis=0)
            out_ref[out_rows, LANES * c:LANES * (c + 1)] = ot.T.astype(BF16)

        dl = dl_ref[...]
        lam = (jnp.exp(jnp.sum(dl[0:1] * dl[1:2], axis=-1, keepdims=True))
               - jnp.exp(jnp.sum(dl[2:3] * dl[3:4], axis=-1, keepdims=True)) + lam_init)

        for c in range(2):
            parts = []
            for par in range(2):
                g1 = GQA_HEADS + 2 * (2 * c + par)
                rows = slice(HEAD_DIM * par, HEAD_DIM * (par + 1))
                parts.append(job_out(g1, rows) - lam * job_out(g1 + 1, rows))
            o = jnp.concatenate(parts, axis=0).T
            sq = o * o
            ms = jnp.where(low_half,
                           jnp.sum(jnp.where(low_half, sq, 0.0), axis=-1, keepdims=True),
                           jnp.sum(jnp.where(low_half, 0.0, sq), axis=-1, keepdims=True)) * (1.0 / HEAD_DIM)
            out_ref[out_rows, GQA_WIDTH + LANES * c:GQA_WIDTH + LANES * (c + 1)] = (
                o * lax.rsqrt(ms + EPS) * dg_ref[...] * (1.0 - lam_init)).astype(BF16)

    bpb = upb // 2
    n_bodies = n_units // 2

    def body(i, carry, **stages):
        (m0, m1), (l0, l1) = carry
        m0, l0 = run(2 * i, 0, m0, l0, **stages)
        m1, l1 = run(2 * i + 1, 1, m1, l1, **stages)
        if isinstance(i, int):
            if i > 1 and (i - 1) % bpb == 0:
                finish_block((i - 1) // bpb - 1)
        elif n_bodies > bpb:
            @pl.when((i - 1) % bpb == 0)
            def _():
                finish_block((i - 1) // bpb - 1)
        return (m0, m1), (l0, l1)

    carry = body(0, ((None, None), (None, None)), do_b=False, do_c=False)
    carry = body(1, (carry[0], (None, None)), do_c=False)
    carry = lax.fori_loop(2, n_bodies, body, carry)
    carry = body(n_bodies, carry, do_a=False)
    body(n_bodies + 1, ((None, None), carry[1]), do_a=False, do_b=False)


def _attention(qx, k, v, dk, dv, caches, dl, dg, *, layer, seq, tq, lam_init):
    pack = ATTN_WIDTH // tq
    n = k.shape[0]
    b = n // seq
    n_ctx = 0 if caches is None else caches[0].shape[2]
    t_all = n_ctx + seq
    per_b = lambda bi: (bi, 0)
    once = pl.Buffered(1) if seq > ATTN_WIDTH else None
    in_specs = [pl.BlockSpec((N_JOBS, seq, LANES), lambda bi: (0, bi, 0)),
                pl.BlockSpec((seq, KV_WIDTH), per_b, pipeline_mode=once),
                pl.BlockSpec((seq, KV_WIDTH), per_b, pipeline_mode=once),
                pl.BlockSpec((seq, DIFF_WIDTH), per_b, pipeline_mode=once),
                pl.BlockSpec((seq, DIFF_WIDTH), per_b, pipeline_mode=once)]
    args = [qx, k, v, dk, dv]
    if n_ctx:
        cache_idx = lambda bi: (bi, layer, 0, 0)
        in_specs += [pl.BlockSpec((None, None, n_ctx, KV_WIDTH), cache_idx),
                     pl.BlockSpec((None, None, n_ctx, KV_WIDTH), cache_idx),
                     pl.BlockSpec((None, None, n_ctx, DIFF_WIDTH), cache_idx),
                     pl.BlockSpec((None, None, n_ctx, DIFF_WIDTH), cache_idx)]
        args += list(caches)
    in_specs += [pl.BlockSpec((None, 4, DIFF_QK_DIM), lambda bi: (layer, 0, 0)),
                 pl.BlockSpec((None, 1, LANES), lambda bi: (layer, 0, 0))]
    args += [dl, dg]
    return pl.pallas_call(
        functools.partial(_attn_kernel, n_ctx=n_ctx, seq=seq, tq=tq, pack=pack, lam_init=lam_init),
        grid=(b,), in_specs=in_specs,
        out_specs=pl.BlockSpec((seq, D_MODEL - LRU_WIDTH), per_b),
        out_shape=jax.ShapeDtypeStruct((n, D_MODEL - LRU_WIDTH), BF16),
        scratch_shapes=[pltpu.VMEM((3, t_all, LANES), BF16), pltpu.VMEM((3, LANES, t_all), BF16),
                        pltpu.VMEM((t_all, ATTN_WIDTH), F32), pltpu.VMEM((t_all, ATTN_WIDTH), F32),
                        pltpu.VMEM((t_all, ATTN_WIDTH), BF16), pltpu.VMEM((t_all, ATTN_WIDTH), BF16),
                        pltpu.VMEM((N_JOBS // pack, LANES, ATTN_WIDTH), F32)],
        compiler_params=_params("arbitrary"), name="attention",
    )(*args)


HALO = 16


def _mix_ffn_kernel(lru_ref, lrup_ref, lrun_ref, att_ref, attp_ref, attn_ref, x_ref, xp_ref, xn_ref,
                    wo_ref, n2_ref, up_ref, cw_ref, cb_ref, wd_ref, mod_ref, fin_ref, o_ref,
                    mix, xe, he, act, *, seq, tm, final):
    i = pl.program_id(0)
    nts = seq // tm
    pos = i % nts
    ne = tm + 2 * HALO
    for r0, nr, lru_r, att_r, x_r in ((0, HALO, lrup_ref, attp_ref, xp_ref),
                                      (HALO, tm, lru_ref, att_ref, x_ref),
                                      (HALO + tm, HALO, lrun_ref, attn_ref, xn_ref)):
        mix[r0:r0 + nr, 0:LRU_WIDTH] = lru_r[...]
        mix[r0:r0 + nr, LRU_WIDTH:D_MODEL] = att_r[...]
        xe[r0:r0 + nr, :] = x_r[...]
    mod = mod_ref[...]
    x1 = xe[...] + mod[:, 2 * D_MODEL:3 * D_MODEL] * jnp.dot(mix[...], wo_ref[...], preferred_element_type=F32)
    xe[...] = x1
    he[...] = (_rms(x1) * (n2_ref[...] * (1.0 + mod[:, 4 * D_MODEL:5 * D_MODEL]))
               + mod[:, 3 * D_MODEL:4 * D_MODEL]).astype(BF16)

    @pl.when(pos == 0)
    def _():
        he[0:HALO, :] = jnp.zeros((HALO, D_MODEL), BF16)

    @pl.when(pos == nts - 1)
    def _():
        he[HALO + tm:ne, :] = jnp.zeros((HALO, D_MODEL), BF16)

    def up(c, half, scale):
        cols = slice(half * D_FF + FFN_CHUNK * c, half * D_FF + FFN_CHUNK * (c + 1))
        u = jnp.dot(he[...], up_ref[:, cols], preferred_element_type=F32)
        cw, cb = cw_ref[:, cols], cb_ref[:, cols]
        if scale != 1.0:
            cw, cb = cw * scale, cb * scale
        return (cw[0:1] * pltpu.roll(u, 1, 0)[HALO:HALO + tm] + cw[1:2] * u[HALO:HALO + tm]
                + cw[2:3] * pltpu.roll(u, ne - 1, 0)[HALO:HALO + tm] + cb)

    for c in range(N_FFN_CHUNKS):
        a, hg = up(c, 0, 1.0), up(c, 1, 0.5)
        act[:, FFN_CHUNK * c:FFN_CHUNK * (c + 1)] = ((hg * jnp.tanh(hg) + hg) * a).astype(BF16)
    down = jnp.dot(act[...], wd_ref[...], preferred_element_type=F32)
    x2 = xe[HALO:HALO + tm, :] + mod[:, 5 * D_MODEL:6 * D_MODEL] * down
    if final:
        x2 = _rms(x2) * fin_ref[...]
    o_ref[...] = x2


def _mix_ffn(lru_o, att_o, x, w_out, n2, up, cw, cb, wd, mod, fin_g, *, layer, seq, tm, mod_row, final):
    n = x.shape[0]
    nts = seq // tm
    nbh = n // HALO
    row = lambda i: (i, 0)
    prev = lambda i: (jnp.maximum(i * (tm // HALO) - 1, 0), 0)
    nxt = lambda i: (jnp.minimum((i + 1) * (tm // HALO), nbh - 1), 0)
    c2 = lambda i: (0, 0)
    per_layer = lambda i: (layer, 0, 0)
    att_w = D_MODEL - LRU_WIDTH

    def with_halo(width):
        return [pl.BlockSpec((tm, width), row), pl.BlockSpec((HALO, width), prev), pl.BlockSpec((HALO, width), nxt)]

    in_specs = (with_halo(LRU_WIDTH) + with_halo(att_w) + with_halo(D_MODEL)
                + [pl.BlockSpec((None, D_MODEL, D_MODEL), per_layer),
                   pl.BlockSpec((None, 1, D_MODEL), per_layer),
                   pl.BlockSpec((None, D_MODEL, 2 * D_FF), per_layer),
                   pl.BlockSpec((None, 3, 2 * D_FF), per_layer),
                   pl.BlockSpec((None, 1, 2 * D_FF), per_layer),
                   pl.BlockSpec((None, D_FF, D_MODEL), per_layer),
                   pl.BlockSpec((None, 1, 6 * D_MODEL), lambda i: (layer * COND_ROWS + mod_row(i // nts), 0, 0)),
                   pl.BlockSpec((1, D_MODEL), c2)])
    return pl.pallas_call(
        functools.partial(_mix_ffn_kernel, seq=seq, tm=tm, final=final),
        grid=(n // tm,), in_specs=in_specs,
        out_specs=pl.BlockSpec((tm, D_MODEL), row),
        out_shape=jax.ShapeDtypeStruct((n, D_MODEL), F32),
        scratch_shapes=[pltpu.VMEM((tm + 2 * HALO, D_MODEL), BF16), pltpu.VMEM((tm + 2 * HALO, D_MODEL), F32),
                        pltpu.VMEM((tm + 2 * HALO, D_MODEL), BF16), pltpu.VMEM((tm, D_FF), BF16)],
        compiler_params=_params("arbitrary"), name="mix_ffn",
    )(lru_o, lru_o, lru_o, att_o, att_o, att_o, x, x, x, w_out, n2, up, cw, cb, wd, mod, fin_g)


def _rope_tables(seq, dim):
    rows = seq // GRID_W
    t_row = np.repeat(np.arange(rows, dtype=np.float64), GRID_W)
    t_col = np.tile(np.arange(GRID_W, dtype=np.float64), rows)
    axis_dim = dim // 2
    inv = ROPE_THETA ** (-np.arange(0, axis_dim, 2, dtype=np.float64) / axis_dim)
    ar = t_row[:, None] * inv
    ac = t_col[:, None] * inv
    ang = np.concatenate([ar, ar, ac, ac], axis=-1)
    reps = LANES // dim
    cos = np.tile(np.cos(ang), (1, reps))
    sin = np.tile(np.sin(ang), (1, reps))
    first = (np.arange(LANES) % (dim // 2)) < (dim // 4)
    sin_neg = np.where(first, -sin, 0.0)
    sin_pos = np.where(first, 0.0, sin)
    return tuple(jnp.asarray(t, F32) for t in (cos, sin_neg, sin_pos))


def _gate_weights(w, b):
    nl = w.shape[0]
    eye = jnp.eye(LRU_BLOCKS, dtype=w.dtype)
    wp = w.reshape(nl, 2, LRU_BLOCKS, LRU_BLOCK_W, 2, LRU_BLOCK_W)
    dense = jnp.einsum('lpndqe,nm->lndpqme', wp, eye).reshape(nl, LRU_WIDTH, 4 * LRU_WIDTH)
    bias = jnp.transpose(b.reshape(nl, 2, LRU_BLOCKS, 2, LRU_BLOCK_W), (0, 1, 3, 2, 4)).reshape(nl, 1, 4 * LRU_WIDTH)
    return (0.5 * dense).astype(BF16), 0.5 * bias


def kernel(x_prompt, x_sample, cache_gqa_k, cache_gqa_v, cache_diff_k, cache_diff_v, state_lru, c, c_ctx,
           norm1_g, norm2_g, final_norm_g, ada_w, ada_b, w_in, w_out, lru_conv_w, lru_conv_b, lru_gate_w,
           lru_gate_b, lru_lambda, gqa_q_norm_g, gqa_k_norm_g, diff_lambda, diff_norm_g, ffn_w_up, ffn_conv_w,
           ffn_conv_b, ffn_w_down):
    bp, sp, _ = x_prompt.shape
    bs, ss, _ = x_sample.shape
    n_ctx = cache_gqa_k.shape[2]

    cond = jnp.concatenate([c_ctx[None, :], c, jnp.zeros((COND_ROWS - 1 - bs, D_MODEL), F32)], axis=0)
    mod_all = _ada(cond, ada_w, ada_b)

    caches = (cache_gqa_k.reshape(bs, DEPTH, n_ctx, KV_WIDTH), cache_gqa_v.reshape(bs, DEPTH, n_ctx, KV_WIDTH),
              cache_diff_k.reshape(bs, DEPTH, n_ctx, DIFF_WIDTH), cache_diff_v.reshape(bs, DEPTH, n_ctx, DIFF_WIDTH))
    tables = _rope_tables(ss, HEAD_DIM) + _rope_tables(ss, DIFF_QK_DIM)
    ones_blk = jnp.asarray(np.kron(np.eye(2), np.ones((HEAD_DIM, HEAD_DIM))), BF16)
    zero_state = jnp.zeros((bp, 1, 2, LRU_WIDTH), F32)

    groups = {
        'p': dict(seq=sp, tm=256, tq=256, mod_row=lambda b: 0, tables=None, caches=None),
        's': dict(seq=ss, tm=512, tq=512, mod_row=lambda b: 1 + b, tables=tables, caches=caches),
    }
    xs = {'p': x_prompt.reshape(bp * sp, D_MODEL), 's': x_sample.reshape(bs * ss, D_MODEL)}
    new_k, new_v, new_dk, new_dv, new_st = [], [], [], [], []

    mod = mod_all.reshape(DEPTH * COND_ROWS, 1, 6 * D_MODEL)
    w_in_b = w_in.astype(BF16)
    w_out_b = w_out.astype(BF16)
    up_b = ffn_w_up.astype(BF16)
    wd_b = ffn_w_down.astype(BF16)
    ffn_cb = ffn_conv_b.reshape(DEPTH, 1, 2 * D_FF)
    fin = final_norm_g[None, :]
    gw, gb = _gate_weights(lru_gate_w, lru_gate_b)
    lru_cb = lru_conv_b.reshape(DEPTH, 1, LRU_WIDTH)
    qg = jnp.tile(gqa_q_norm_g, (1, 2)).reshape(DEPTH, 1, LANES)
    kg = jnp.tile(gqa_k_norm_g, (1, 2)).reshape(DEPTH, 1, LANES)
    dg = jnp.tile(diff_norm_g, (1, 2)).reshape(DEPTH, 1, LANES)
    n1 = norm1_g.reshape(DEPTH, 1, D_MODEL)
    n2 = norm2_g.reshape(DEPTH, 1, D_MODEL)

    for l in range(DEPTH):
        lam_init = 0.8 - 0.6 * math.exp(-0.3 * l)
        for name in ('p', 's'):
            g = groups[name]
            seq, tm, mod_row = g['seq'], g['tm'], g['mod_row']
            x = xs[name]
            lru, qx, k, v, dk, dv = _norm_proj(x, mod, n1, w_in_b, qg, kg, ones_blk, g['tables'],
                                               layer=l, seq=seq, tm=512, mod_row=mod_row)
            h0, state_layer = (zero_state, 0) if name == 'p' else (state_lru, l)
            lru_o, st = _lru(lru, lru_conv_w, lru_cb, gw, gb, lru_lambda, h0,
                             layer=l, state_layer=state_layer, seq=seq, tr=min(seq, 512))
            att_o = _attention(qx, k, v, dk, dv, g['caches'], diff_lambda, dg,
                               layer=l, seq=seq, tq=g['tq'], lam_init=lam_init)
            xs[name] = _mix_ffn(lru_o, att_o, x, w_out_b, n2, up_b, ffn_conv_w, ffn_cb, wd_b, mod, fin,
                                layer=l, seq=seq, tm=tm, mod_row=mod_row, final=(l == DEPTH - 1))
            if name == 'p':
                new_k.append(k)
                new_v.append(v)
                new_dk.append(dk)
                new_dv.append(dv)
                new_st.append(st)

    y_prompt = xs['p'].reshape(bp, sp, D_MODEL)
    y_sample = xs['s'].reshape(bs, ss, D_MODEL)
    stack = lambda parts, shape: jnp.stack([p.reshape((bp,) + shape) for p in parts], axis=1)
    return (y_prompt, y_sample,
            stack(new_k, (sp, GQA_KV_HEADS, HEAD_DIM)),
            stack(new_v, (sp, GQA_KV_HEADS, HEAD_DIM)),
            stack(new_dk, (sp, DIFF_HEADS, 2, DIFF_QK_DIM)),
            stack(new_dv, (sp, DIFF_HEADS, HEAD_DIM)),
            jnp.stack(new_st, axis=1))
```

```python
import functools
import math

import numpy as np
import jax
import jax.numpy as jnp
from jax import lax
from jax.experimental import pallas as pl
from jax.experimental.pallas import tpu as pltpu

F32 = jnp.float32
BF16 = jnp.bfloat16

D_MODEL = 1024
DEPTH = 2
GRID_W = 64
HEAD_DIM = 64
LRU_WIDTH = 256
LRU_BLOCKS = 4
LRU_BLOCK_W = LRU_WIDTH // LRU_BLOCKS
LRU_C = 8.0
GQA_HEADS = 8
GQA_KV_HEADS = 2
DIFF_HEADS = 4
DIFF_QK_DIM = 32
D_FF = 2816
ROPE_THETA = 10000.0
EPS = 1e-6

LRU_IN = 2 * LRU_WIDTH
GQA_WIDTH = GQA_HEADS * HEAD_DIM
KV_WIDTH = GQA_KV_HEADS * HEAD_DIM
DIFF_WIDTH = DIFF_HEADS * HEAD_DIM
Q_OFF = LRU_IN
KV_OFF = Q_OFF + GQA_WIDTH
IN_WIDTH = KV_OFF + 2 * KV_WIDTH + 3 * DIFF_WIDTH
LOG2_E = math.log2(math.e)
N_JOBS = GQA_HEADS + 2 * DIFF_HEADS
KEY_CHUNK = 256
ATTN_WIDTH = 512
LANES = 128
FFN_CHUNK = 256
N_FFN_CHUNKS = D_FF // FFN_CHUNK
COND_ROWS = 8
VMEM_LIMIT = 56 * 2 ** 20


def _params(*sem):
    return pltpu.CompilerParams(dimension_semantics=sem, vmem_limit_bytes=VMEM_LIMIT)


def _rms(x):
    return x * lax.rsqrt(jnp.mean(x * x, axis=-1, keepdims=True) + EPS)


def _ada_kernel(cond_ref, w_ref, b_ref, o_ref):
    c = cond_ref[...]
    s = c * jax.nn.sigmoid(c)
    o_ref[...] = jnp.dot(s.astype(BF16), w_ref[...].astype(BF16), preferred_element_type=F32) + b_ref[...]


def _ada(cond, ada_w, ada_b):
    tn = 1536
    width = 6 * D_MODEL
    return pl.pallas_call(
        _ada_kernel,
        grid=(DEPTH, width // tn),
        in_specs=[pl.BlockSpec((COND_ROWS, D_MODEL), lambda l, j: (0, 0)),
                  pl.BlockSpec((None, D_MODEL, tn), lambda l, j: (l, 0, j)),
                  pl.BlockSpec((None, 1, tn), lambda l, j: (l, 0, j))],
        out_specs=pl.BlockSpec((None, COND_ROWS, tn), lambda l, j: (l, 0, j)),
        out_shape=jax.ShapeDtypeStruct((DEPTH, COND_ROWS, width), F32),
        compiler_params=_params("arbitrary", "arbitrary"),
        name="ada",
    )(cond, ada_w, ada_b.reshape(DEPTH, 1, width))


def _group_mean_sq(x, ones_blk):
    sq = x * x
    hi = sq.astype(BF16)
    lo = (sq - hi.astype(F32)).astype(BF16)
    s = jnp.dot(hi, ones_blk, preferred_element_type=F32) + jnp.dot(lo, ones_blk, preferred_element_type=F32)
    return s * (1.0 / HEAD_DIM)


def _rope(x, cos, sin_neg, sin_pos, quarter):
    return x * cos + pltpu.roll(x, LANES - quarter, 1) * sin_neg + pltpu.roll(x, quarter, 1) * sin_pos


def _norm_proj_kernel(*refs, rope):
    if rope:
        (x_ref, mod_ref, n1_ref, w_ref, qg_ref, kg_ref, ones_ref, cq, snq, spq, cd, snd, spd,
         lru_ref, qx_ref, k_ref, v_ref, dk_ref, dv_ref) = refs
    else:
        (x_ref, mod_ref, n1_ref, w_ref, qg_ref, kg_ref, ones_ref,
         lru_ref, qx_ref, k_ref, v_ref, dk_ref, dv_ref) = refs
    mod = mod_ref[...]
    h = _rms(x_ref[...]) * (n1_ref[...] * (1.0 + mod[:, D_MODEL:2 * D_MODEL])) + mod[:, 0:D_MODEL]
    hb = h.astype(BF16)
    ones_blk = ones_ref[...]
    lane = lax.broadcasted_iota(jnp.int32, (1, LANES), 1)

    proj = jnp.dot(hb, w_ref[:, Q_OFF:KV_OFF], preferred_element_type=F32)
    for c in range(4):
        xc = proj[:, LANES * c:LANES * (c + 1)]
        xc = xc * lax.rsqrt(_group_mean_sq(xc, ones_blk) + EPS) * qg_ref[...]
        if rope:
            xc = _rope(xc, cq[...], snq[...], spq[...], HEAD_DIM // 4)
        xc = xc * (HEAD_DIM ** -0.5 * LOG2_E)
        xr = pltpu.roll(xc, HEAD_DIM, 1)
        for par in range(2):
            j = 2 * c + par
            want = j // (GQA_HEADS // GQA_KV_HEADS)
            src = xc if par == want else xr
            qx_ref[j] = jnp.where(lane // HEAD_DIM == want, src, 0.0).astype(BF16)

    proj = jnp.dot(hb, w_ref[:, KV_OFF:IN_WIDTH], preferred_element_type=F32)
    dq_off = 2 * KV_WIDTH
    dk_off = dq_off + DIFF_WIDTH
    dv_off = dk_off + DIFF_WIDTH
    kc = proj[:, 0:KV_WIDTH]
    kc = kc * lax.rsqrt(_group_mean_sq(kc, ones_blk) + EPS) * kg_ref[...]
    if rope:
        kc = _rope(kc, cq[...], snq[...], spq[...], HEAD_DIM // 4)
    k_ref[...] = kc
    v_ref[...] = proj[:, KV_WIDTH:2 * KV_WIDTH]

    for c in range(2):
        xc = proj[:, dq_off + LANES * c:dq_off + LANES * (c + 1)]
        if rope:
            xc = _rope(xc, cd[...], snd[...], spd[...], DIFF_QK_DIM // 4)
        xc = xc * (DIFF_QK_DIM ** -0.5 * LOG2_E)
        for g in range(4):
            qx_ref[GQA_HEADS + 4 * c + g] = jnp.where(lane // DIFF_QK_DIM == g, xc, 0.0).astype(BF16)
        kc = proj[:, dk_off + LANES * c:dk_off + LANES * (c + 1)]
        if rope:
            kc = _rope(kc, cd[...], snd[...], spd[...], DIFF_QK_DIM // 4)
        dk_ref[:, LANES * c:LANES * (c + 1)] = kc
    dv_ref[...] = proj[:, dv_off:dv_off + DIFF_WIDTH]

    lru_ref[...] = jnp.dot(hb, w_ref[:, 0:LRU_IN], preferred_element_type=F32)


def _norm_proj(x, mod, n1, w_in, qg, kg, ones_blk, tables, *, layer, seq, tm, mod_row):
    n = x.shape[0]
    rope = tables is not None
    nts = seq // tm
    row = lambda i: (i, 0)
    const = lambda i: (0, 0)
    per_layer = lambda i: (layer, 0, 0)
    in_specs = [pl.BlockSpec((tm, D_MODEL), row),
                pl.BlockSpec((None, 1, 6 * D_MODEL), lambda i: (layer * COND_ROWS + mod_row(i * tm // seq), 0, 0)),
                pl.BlockSpec((None, 1, D_MODEL), per_layer),
                pl.BlockSpec((None, D_MODEL, IN_WIDTH), per_layer),
                pl.BlockSpec((None, 1, LANES), per_layer),
                pl.BlockSpec((None, 1, LANES), per_layer),
                pl.BlockSpec((LANES, LANES), const)]
    args = [x, mod, n1, w_in, qg, kg, ones_blk]
    if rope:
        in_specs += [pl.BlockSpec((tm, LANES), lambda i: (i % nts, 0))] * 6
        args += list(tables)
    slab = lambda i: (0, i, 0)
    widths = (LRU_IN, KV_WIDTH, KV_WIDTH, DIFF_WIDTH, DIFF_WIDTH)
    out_specs = [pl.BlockSpec((tm, wd), row) for wd in widths]
    out_shape = [jax.ShapeDtypeStruct((n, wd), F32) for wd in widths]
    out_specs.insert(1, pl.BlockSpec((N_JOBS, tm, LANES), slab))
    out_shape.insert(1, jax.ShapeDtypeStruct((N_JOBS, n, LANES), BF16))
    return pl.pallas_call(
        functools.partial(_norm_proj_kernel, rope=rope),
        grid=(n // tm,), in_specs=in_specs, out_specs=out_specs, out_shape=out_shape,
        compiler_params=_params("arbitrary"), name="norm_proj",
    )(*args)


def _chunk_scan(a, u, row, reverse):
    for d in (1, 2, 4):
        shift = 8 - d if reverse else d
        a_s = pltpu.roll(a, shift, 0)
        u_s = pltpu.roll(u, shift, 0)
        m = (row < 8 - d) if reverse else (row >= d)
        u = jnp.where(m, a * u_s + u, u)
        a = jnp.where(m, a * a_s, a)
    return a, u


def _lru_kernel(x_ref, xp_ref, xn_ref, cw_ref, cb_ref, gw_ref, gb_ref, lam_ref, h0_ref, out_ref, st_ref,
                af, uf, ab, ub, gg, *, seq, tr):
    j = pl.program_id(1)
    nt = seq // tr
    xg = x_ref[...]
    x = xg[:, 0:LRU_WIDTH]
    prev = jnp.where(j > 0, xp_ref[...], 0.0)
    nxt = jnp.where(j < nt - 1, xn_ref[...], 0.0)
    xe = jnp.concatenate([prev, x, nxt], axis=0)
    ne = tr + 16
    cw = cw_ref[...]
    xc = (cw[0:1] * pltpu.roll(xe, 2, 0)[8:8 + tr] + cw[1:2] * pltpu.roll(xe, 1, 0)[8:8 + tr]
          + cw[2:3] * x + cw[3:4] * pltpu.roll(xe, ne - 1, 0)[8:8 + tr] + cb_ref[...])
    sg = 0.5 * jnp.tanh(jnp.dot(xc.astype(BF16), gw_ref[...], preferred_element_type=F32) + gb_ref[...]) + 0.5
    z = -lam_ref[...]
    neg_c_softplus = -LRU_C * (jnp.maximum(z, 0.0) + jnp.log1p(jnp.exp(-jnp.abs(z))))
    r0 = pl.multiple_of(j * tr, tr)
    for d, (a_s, u_s) in enumerate(((af, uf), (ab, ub))):
        r = sg[:, 2 * LRU_WIDTH * d:2 * LRU_WIDTH * d + LRU_WIDTH]
        i = sg[:, 2 * LRU_WIDTH * d + LRU_WIDTH:2 * LRU_WIDTH * (d + 1)]
        log_a = neg_c_softplus[d:d + 1] * r
        a = jnp.exp(log_a)
        a_s[pl.ds(r0, tr), :] = a
        u_s[pl.ds(r0, tr), :] = jnp.sqrt(-jnp.tanh(log_a) * (a * a + 1.0)) * (i * xc)
    gg[pl.ds(r0, tr), :] = jax.nn.gelu(xg[:, LRU_WIDTH:2 * LRU_WIDTH])

    @pl.when(j == nt - 1)
    def _():
        row = lax.broadcasted_iota(jnp.int32, (8, LRU_WIDTH), 0)
        nchunk = seq // 8

        def body(c, carry):
            hf, hb = carry
            rf = pl.multiple_of(c * 8, 8)
            a, u = _chunk_scan(af[pl.ds(rf, 8), :], uf[pl.ds(rf, 8), :], row, False)
            hs = u + a * hf
            uf[pl.ds(rf, 8), :] = hs
            rb = pl.multiple_of((nchunk - 1 - c) * 8, 8)
            a2, u2 = _chunk_scan(ab[pl.ds(rb, 8), :], ub[pl.ds(rb, 8), :], row, True)
            hs2 = u2 + a2 * hb
            ub[pl.ds(rb, 8), :] = hs2
            return hs[7:8, :], hs2[0:1, :]

        h0 = h0_ref[...]
        hf, hb = lax.fori_loop(0, nchunk, body, (h0[0:1], h0[1:2]), unroll=4)
        st_ref[...] = jnp.concatenate([hf, hb], axis=0)

        def obody(t, carry):
            r = pl.multiple_of(t * tr, tr)
            out_ref[pl.ds(r, tr), :] = (gg[pl.ds(r, tr), :] * (uf[pl.ds(r, tr), :] + ub[pl.ds(r, tr), :])).astype(BF16)
            return carry

        lax.fori_loop(0, nt, obody, 0)


def _lru(lru, cw, cb, gw, gb, lam, h0, *, layer, state_layer, seq, tr):
    n = lru.shape[0]
    b = n // seq
    nt = seq // tr
    nb8 = n // 8
    per_layer = lambda bi, j: (layer, 0, 0)
    in_specs = [pl.BlockSpec((tr, LRU_IN), lambda bi, j: (bi * nt + j, 0)),
                pl.BlockSpec((8, LRU_WIDTH), lambda bi, j: (jnp.maximum((bi * nt + j) * (tr // 8) - 1, 0), 0)),
                pl.BlockSpec((8, LRU_WIDTH), lambda bi, j: (jnp.minimum((bi * nt + j + 1) * (tr // 8), nb8 - 1), 0)),
                pl.BlockSpec((None, 4, LRU_WIDTH), per_layer),
                pl.BlockSpec((None, 1, LRU_WIDTH), per_layer),
                pl.BlockSpec((None, LRU_WIDTH, 4 * LRU_WIDTH), per_layer),
                pl.BlockSpec((None, 1, 4 * LRU_WIDTH), per_layer),
                pl.BlockSpec((None, 2, LRU_WIDTH), per_layer),
                pl.BlockSpec((None, None, 2, LRU_WIDTH), lambda bi, j: (bi, state_layer, 0, 0))]
    out_specs = [pl.BlockSpec((seq, LRU_WIDTH), lambda bi, j: (bi, 0)),
                 pl.BlockSpec((None, 2, LRU_WIDTH), lambda bi, j: (bi, 0, 0))]
    out_shape = [jax.ShapeDtypeStruct((n, LRU_WIDTH), BF16),
                 jax.ShapeDtypeStruct((b, 2, LRU_WIDTH), F32)]
    return pl.pallas_call(
        functools.partial(_lru_kernel, seq=seq, tr=tr),
        grid=(b, nt), in_specs=in_specs, out_specs=out_specs, out_shape=out_shape,
        scratch_shapes=[pltpu.VMEM((seq, LRU_WIDTH), F32)] * 5,
        compiler_params=_params("arbitrary", "arbitrary"), name="lru",
    )(lru, lru, lru, cw, cb, gw, gb, lam, h0)


_NT = (((1,), (1,)), ((), ()))


def _attn_kernel(*refs, n_ctx, seq, nseq, tq, pack, ways, lam_init):
    n_in = 11 if n_ctx else 7
    if n_ctx:
        q_ref, k_ref, v_ref, dk_ref, dv_ref, ck_ref, cv_ref, cdk_ref, cdv_ref, dl_ref, dg_ref = refs[:n_in]
    else:
        q_ref, k_ref, v_ref, dk_ref, dv_ref, dl_ref, dg_ref = refs[:n_in]
    out_ref, ks, vts = refs[n_in:n_in + 3]
    s_bufs = refs[n_in + 3:n_in + 3 + ways]
    e_bufs = refs[n_in + 3 + ways:n_in + 3 + 2 * ways]
    o_buf = refs[n_in + 3 + 2 * ways]
    t_all = n_ctx + seq
    n_chunks = t_all // KEY_CHUNK

    def put(slab, new, ctx):
        if n_ctx:
            ks[slab, 0:n_ctx, :] = ctx[0].astype(BF16)
            vts[slab, :, 0:n_ctx] = ctx[1].T.astype(BF16)
        ks[slab, n_ctx:t_all, :] = new[0].astype(BF16)
        vts[slab, :, n_ctx:t_all] = new[1].T.astype(BF16)

    for s in range(nseq):
        rs = slice(seq * s, seq * (s + 1))
        put(3 * s, (k_ref[rs, :], v_ref[rs, :]), (ck_ref[...], cv_ref[...]) if n_ctx else None)
        for c in range(2):
            cols = slice(LANES * c, LANES * (c + 1))
            put(3 * s + 1 + c, (dk_ref[rs, cols], dv_ref[rs, cols]),
                (cdk_ref[:, cols], cdv_ref[:, cols]) if n_ctx else None)

    width = pack * tq
    upb = N_JOBS // pack
    blocks_per_seq = seq // tq
    n_units = upb * blocks_per_seq * nseq

    def slab_of(t):
        g = (t % upb) * pack
        return 3 * (t // (upb * blocks_per_seq)) + jnp.where(g < GQA_HEADS, 0, 1 + (g - GQA_HEADS) // 4)

    def q_unit(t):
        u, qb = t % upb, t // upb
        r0 = qb * tq if isinstance(t, int) else pl.multiple_of(qb * tq, tq)
        if pack == 1:
            return q_ref[u, pl.ds(r0, tq), :]
        return q_ref[pl.ds(u * pack, pack), pl.ds(r0, tq), :].reshape(width, LANES)

    def fold8(x, op):
        acc = x[0:8]
        for r in range(1, KEY_CHUNK // 8):
            acc = op(acc, x[8 * r:8 * (r + 1)])
        return acc

    def run(t, par, m_prev, l_prev, do_a=True, do_b=True, do_c=True):
        ga, gb, gc = t, t - ways, t - 2 * ways
        s_buf, e_buf = s_bufs[par], e_bufs[par]
        m_acc = l_acc = None
        if do_a:
            qa = q_unit(ga)
            ka = slab_of(ga)
            m_acc = jnp.full((8, width), -jnp.inf, F32)
        if do_b:
            m_row = jnp.max(m_prev, axis=0, keepdims=True)
            l_acc = jnp.zeros((8, width), F32)
        if do_c:
            vc = slab_of(gc)
            o_acc = jnp.zeros((LANES, width), F32)
        for c in range(n_chunks):
            rows = slice(KEY_CHUNK * c, KEY_CHUNK * (c + 1))
            if do_c:
                o_acc = o_acc + jnp.dot(vts[vc, :, rows], e_buf[rows, :], preferred_element_type=F32)
            if do_b:
                e = jnp.exp2(s_buf[rows, :] - m_row)
                e_buf[rows, :] = e.astype(BF16)
                l_acc = l_acc + fold8(e, jnp.add)
            if do_a:
                s = lax.dot_general(ks[ka, rows, :], qa, _NT, preferred_element_type=F32)
                s_buf[rows, :] = s
                m_acc = jnp.maximum(m_acc, fold8(s, jnp.maximum))
        if do_c:
            o_buf[gc % upb] = o_acc * (1.0 / jnp.sum(l_prev, axis=0, keepdims=True))
        return m_acc, l_acc

    def job_out(g, rows):
        return o_buf[g // pack, rows, tq * (g % pack):tq * (g % pack + 1)]

    def finish_block(qb):
        r0 = qb * tq if isinstance(qb, int) else pl.multiple_of(qb * tq, tq)
        out_rows = pl.ds(r0, tq)
        for c in range(4):
            h = c // 2
            rows = slice(HEAD_DIM * h, HEAD_DIM * (h + 1))
            ot = jnp.concatenate([job_out(2 * c, rows), job_out(2 * c + 1, rows)], axis=0)
            out_ref[out_rows, LANES * c:LANES * (c + 1)] = ot.astype(BF16).T

        dl = dl_ref[...]
        lam = (jnp.exp(jnp.sum(dl[0:1] * dl[1:2], axis=-1, keepdims=True))
               - jnp.exp(jnp.sum(dl[2:3] * dl[3:4], axis=-1, keepdims=True)) + lam_init)

        gain = dg_ref[...] * (1.0 - lam_init)
        for c in range(2):
            parts = []
            for par in range(2):
                g1 = GQA_HEADS + 2 * (2 * c + par)
                rows = slice(HEAD_DIM * par, HEAD_DIM * (par + 1))
                o = job_out(g1, rows) - lam * job_out(g1 + 1, rows)
                parts.append(o * lax.rsqrt(jnp.mean(o * o, axis=0, keepdims=True) + EPS))
            ot = jnp.concatenate(parts, axis=0)
            ot = jnp.concatenate([ot[:, LANES * j:LANES * (j + 1)] * gain for j in range(tq // LANES)], axis=1)
            out_ref[out_rows, GQA_WIDTH + LANES * c:GQA_WIDTH + LANES * (c + 1)] = ot.astype(BF16).T

    bpb = upb // ways
    n_bodies = n_units // ways
    nothing = (None,) * ways

    def body(i, carry, **stages):
        ms, ls = [], []
        for par in range(ways):
            m, l = run(ways * i + par, par, carry[0][par], carry[1][par], **stages)
            ms.append(m)
            ls.append(l)
        if isinstance(i, int):
            if i > 1 and (i - 1) % bpb == 0:
                finish_block((i - 1) // bpb - 1)
        elif n_bodies > bpb:
            @pl.when((i - 1) % bpb == 0)
            def _():
                finish_block((i - 1) // bpb - 1)
        return tuple(ms), tuple(ls)

    carry = body(0, (nothing, nothing), do_b=False, do_c=False)
    carry = body(1, (carry[0], nothing), do_c=False)
    carry = lax.fori_loop(2, n_bodies, body, carry)
    carry = body(n_bodies, carry, do_a=False)
    body(n_bodies + 1, (nothing, carry[1]), do_a=False, do_b=False)


def _attention(qx, k, v, dk, dv, caches, dl, dg, *, layer, seq, nseq, tq, lam_init):
    pack = ATTN_WIDTH // tq
    ways = 2 if seq > ATTN_WIDTH else 4
    n = k.shape[0]
    rows = nseq * seq
    n_ctx = 0 if caches is None else caches[0].shape[2]
    assert n_ctx == 0 or nseq == 1
    t_all = n_ctx + seq
    per_b = lambda bi: (bi, 0)
    once = pl.Buffered(1) if seq > ATTN_WIDTH else None
    in_specs = [pl.BlockSpec((N_JOBS, rows, LANES), lambda bi: (0, bi, 0)),
                pl.BlockSpec((rows, KV_WIDTH), per_b, pipeline_mode=once),
                pl.BlockSpec((rows, KV_WIDTH), per_b, pipeline_mode=once),
                pl.BlockSpec((rows, DIFF_WIDTH), per_b, pipeline_mode=once),
                pl.BlockSpec((rows, DIFF_WIDTH), per_b, pipeline_mode=once)]
    args = [qx, k, v, dk, dv]
    if n_ctx:
        cache_idx = lambda bi: (bi, layer, 0, 0)
        in_specs += [pl.BlockSpec((None, None, n_ctx, KV_WIDTH), cache_idx),
                     pl.BlockSpec((None, None, n_ctx, KV_WIDTH), cache_idx),
                     pl.BlockSpec((None, None, n_ctx, DIFF_WIDTH), cache_idx),
                     pl.BlockSpec((None, None, n_ctx, DIFF_WIDTH), cache_idx)]
        args += list(caches)
    in_specs += [pl.BlockSpec((None, 4, DIFF_QK_DIM), lambda bi: (layer, 0, 0)),
                 pl.BlockSpec((None, LANES, LANES), lambda bi: (layer, 0, 0))]
    args += [dl, dg]
    return pl.pallas_call(
        functools.partial(_attn_kernel, n_ctx=n_ctx, seq=seq, nseq=nseq, tq=tq, pack=pack, ways=ways,
                          lam_init=lam_init),
        grid=(n // rows,), in_specs=in_specs,
        out_specs=pl.BlockSpec((rows, D_MODEL - LRU_WIDTH), per_b),
        out_shape=jax.ShapeDtypeStruct((n, D_MODEL - LRU_WIDTH), BF16),
        scratch_shapes=[pltpu.VMEM((3 * nseq, t_all, LANES), BF16), pltpu.VMEM((3 * nseq, LANES, t_all), BF16)]
        + [pltpu.VMEM((t_all, ATTN_WIDTH), F32)] * ways + [pltpu.VMEM((t_all, ATTN_WIDTH), BF16)] * ways
        + [pltpu.VMEM((N_JOBS // pack, LANES, ATTN_WIDTH), F32)],
        compiler_params=_params("arbitrary"), name="attention",
    )(*args)


HALO = 16


def _mix_ffn_kernel(lru_ref, lrup_ref, lrun_ref, att_ref, attp_ref, attn_ref, x_ref, xp_ref, xn_ref,
                    wo_ref, n2_ref, up_ref, cw_ref, cb_ref, wd_ref, mod_ref, fin_ref, o_ref,
                    mix, xe, he, act, *, seq, tm, final):
    i = pl.program_id(0)
    nts = seq // tm
    pos = i % nts
    ne = tm + 2 * HALO
    for r0, nr, lru_r, att_r, x_r in ((0, HALO, lrup_ref, attp_ref, xp_ref),
                                      (HALO, tm, lru_ref, att_ref, x_ref),
                                      (HALO + tm, HALO, lrun_ref, attn_ref, xn_ref)):
        mix[r0:r0 + nr, 0:LRU_WIDTH] = lru_r[...]
        mix[r0:r0 + nr, LRU_WIDTH:D_MODEL] = att_r[...]
        xe[r0:r0 + nr, :] = x_r[...]
    mod = mod_ref[...]
    x1 = xe[...] + mod[:, 2 * D_MODEL:3 * D_MODEL] * jnp.dot(mix[...], wo_ref[...], preferred_element_type=F32)
    xe[...] = x1
    he[...] = (_rms(x1) * (n2_ref[...] * (1.0 + mod[:, 4 * D_MODEL:5 * D_MODEL]))
               + mod[:, 3 * D_MODEL:4 * D_MODEL]).astype(BF16)

    @pl.when(pos == 0)
    def _():
        he[0:HALO, :] = jnp.zeros((HALO, D_MODEL), BF16)

    @pl.when(pos == nts - 1)
    def _():
        he[HALO + tm:ne, :] = jnp.zeros((HALO, D_MODEL), BF16)

    def up(c, half, scale):
        cols = slice(half * D_FF + FFN_CHUNK * c, half * D_FF + FFN_CHUNK * (c + 1))
        u = jnp.dot(he[...], up_ref[:, cols], preferred_element_type=F32)
        cw, cb = cw_ref[:, cols], cb_ref[:, cols]
        if scale != 1.0:
            cw, cb = cw * scale, cb * scale
        return (cw[0:1] * pltpu.roll(u, 1, 0)[HALO:HALO + tm] + cw[1:2] * u[HALO:HALO + tm]
                + cw[2:3] * pltpu.roll(u, ne - 1, 0)[HALO:HALO + tm] + cb)

    for c in range(N_FFN_CHUNKS):
        a, hg = up(c, 0, 1.0), up(c, 1, 0.5)
        act[:, FFN_CHUNK * c:FFN_CHUNK * (c + 1)] = ((hg * jnp.tanh(hg) + hg) * a).astype(BF16)
    down = jnp.dot(act[...], wd_ref[...], preferred_element_type=F32)
    x2 = xe[HALO:HALO + tm, :] + mod[:, 5 * D_MODEL:6 * D_MODEL] * down
    if final:
        x2 = _rms(x2) * fin_ref[...]
    o_ref[...] = x2


def _mix_ffn(lru_o, att_o, x, w_out, n2, up, cw, cb, wd, mod, fin_g, *, layer, seq, tm, mod_row, final):
    n = x.shape[0]
    nts = seq // tm
    nbh = n // HALO
    row = lambda i: (i, 0)
    prev = lambda i: (jnp.maximum(i * (tm // HALO) - 1, 0), 0)
    nxt = lambda i: (jnp.minimum((i + 1) * (tm // HALO), nbh - 1), 0)
    c2 = lambda i: (0, 0)
    per_layer = lambda i: (layer, 0, 0)
    att_w = D_MODEL - LRU_WIDTH

    def with_halo(width):
        return [pl.BlockSpec((tm, width), row), pl.BlockSpec((HALO, width), prev), pl.BlockSpec((HALO, width), nxt)]

    in_specs = (with_halo(LRU_WIDTH) + with_halo(att_w) + with_halo(D_MODEL)
                + [pl.BlockSpec((None, D_MODEL, D_MODEL), per_layer),
                   pl.BlockSpec((None, 1, D_MODEL), per_layer),
                   pl.BlockSpec((None, D_MODEL, 2 * D_FF), per_layer),
                   pl.BlockSpec((None, 3, 2 * D_FF), per_layer),
                   pl.BlockSpec((None, 1, 2 * D_FF), per_layer),
                   pl.BlockSpec((None, D_FF, D_MODEL), per_layer),
                   pl.BlockSpec((None, 1, 6 * D_MODEL), lambda i: (layer * COND_ROWS + mod_row(i // nts), 0, 0)),
                   pl.BlockSpec((1, D_MODEL), c2)])
    return pl.pallas_call(
        functools.partial(_mix_ffn_kernel, seq=seq, tm=tm, final=final),
        grid=(n // tm,), in_specs=in_specs,
        out_specs=pl.BlockSpec((tm, D_MODEL), row),
        out_shape=jax.ShapeDtypeStruct((n, D_MODEL), F32),
        scratch_shapes=[pltpu.VMEM((tm + 2 * HALO, D_MODEL), BF16), pltpu.VMEM((tm + 2 * HALO, D_MODEL), F32),
                        pltpu.VMEM((tm + 2 * HALO, D_MODEL), BF16), pltpu.VMEM((tm, D_FF), BF16)],
        compiler_params=_params("arbitrary"), name="mix_ffn",
    )(lru_o, lru_o, lru_o, att_o, att_o, att_o, x, x, x, w_out, n2, up, cw, cb, wd, mod, fin_g)


def _rope_tables(seq, dim):
    rows = seq // GRID_W
    t_row = np.repeat(np.arange(rows, dtype=np.float64), GRID_W)
    t_col = np.tile(np.arange(GRID_W, dtype=np.float64), rows)
    axis_dim = dim // 2
    inv = ROPE_THETA ** (-np.arange(0, axis_dim, 2, dtype=np.float64) / axis_dim)
    ar = t_row[:, None] * inv
    ac = t_col[:, None] * inv
    ang = np.concatenate([ar, ar, ac, ac], axis=-1)
    reps = LANES // dim
    cos = np.tile(np.cos(ang), (1, reps))
    sin = np.tile(np.sin(ang), (1, reps))
    first = (np.arange(LANES) % (dim // 2)) < (dim // 4)
    sin_neg = np.where(first, -sin, 0.0)
    sin_pos = np.where(first, 0.0, sin)
    return tuple(jnp.asarray(t, F32) for t in (cos, sin_neg, sin_pos))


def _gate_weights(w, b):
    nl = w.shape[0]
    eye = jnp.eye(LRU_BLOCKS, dtype=w.dtype)
    wp = w.reshape(nl, 2, LRU_BLOCKS, LRU_BLOCK_W, 2, LRU_BLOCK_W)
    dense = jnp.einsum('lpndqe,nm->lndpqme', wp, eye).reshape(nl, LRU_WIDTH, 4 * LRU_WIDTH)
    bias = jnp.transpose(b.reshape(nl, 2, LRU_BLOCKS, 2, LRU_BLOCK_W), (0, 1, 3, 2, 4)).reshape(nl, 1, 4 * LRU_WIDTH)
    return (0.5 * dense).astype(BF16), 0.5 * bias


def kernel(x_prompt, x_sample, cache_gqa_k, cache_gqa_v, cache_diff_k, cache_diff_v, state_lru, c, c_ctx,
           norm1_g, norm2_g, final_norm_g, ada_w, ada_b, w_in, w_out, lru_conv_w, lru_conv_b, lru_gate_w,
           lru_gate_b, lru_lambda, gqa_q_norm_g, gqa_k_norm_g, diff_lambda, diff_norm_g, ffn_w_up, ffn_conv_w,
           ffn_conv_b, ffn_w_down):
    bp, sp, _ = x_prompt.shape
    bs, ss, _ = x_sample.shape
    n_ctx = cache_gqa_k.shape[2]

    cond = jnp.concatenate([c_ctx[None, :], c, jnp.zeros((COND_ROWS - 1 - bs, D_MODEL), F32)], axis=0)
    mod_all = _ada(cond, ada_w, ada_b)

    caches = (cache_gqa_k.reshape(bs, DEPTH, n_ctx, KV_WIDTH), cache_gqa_v.reshape(bs, DEPTH, n_ctx, KV_WIDTH),
              cache_diff_k.reshape(bs, DEPTH, n_ctx, DIFF_WIDTH), cache_diff_v.reshape(bs, DEPTH, n_ctx, DIFF_WIDTH))
    tables = _rope_tables(ss, HEAD_DIM) + _rope_tables(ss, DIFF_QK_DIM)
    ones_blk = jnp.asarray(np.kron(np.eye(2), np.ones((HEAD_DIM, HEAD_DIM))), BF16)
    zero_state = jnp.zeros((bp, 1, 2, LRU_WIDTH), F32)

    groups = {
        'p': dict(seq=sp, tm=256, tq=256, nseq=4, mod_row=lambda b: 0, tables=None, caches=None),
        's': dict(seq=ss, tm=512, tq=512, nseq=1, mod_row=lambda b: 1 + b, tables=tables, caches=caches),
    }
    xs = {'p': x_prompt.reshape(bp * sp, D_MODEL), 's': x_sample.reshape(bs * ss, D_MODEL)}
    new_k, new_v, new_dk, new_dv, new_st = [], [], [], [], []

    mod = mod_all.reshape(DEPTH * COND_ROWS, 1, 6 * D_MODEL)
    w_in_b = w_in.astype(BF16)
    w_out_b = w_out.astype(BF16)
    up_b = ffn_w_up.astype(BF16)
    wd_b = ffn_w_down.astype(BF16)
    ffn_cb = ffn_conv_b.reshape(DEPTH, 1, 2 * D_FF)
    fin = final_norm_g[None, :]
    gw, gb = _gate_weights(lru_gate_w, lru_gate_b)
    lru_cb = lru_conv_b.reshape(DEPTH, 1, LRU_WIDTH)
    qg = jnp.tile(gqa_q_norm_g, (1, 2)).reshape(DEPTH, 1, LANES)
    kg = jnp.tile(gqa_k_norm_g, (1, 2)).reshape(DEPTH, 1, LANES)
    dg = jnp.broadcast_to(jnp.tile(diff_norm_g, (1, 2))[:, :, None], (DEPTH, LANES, LANES))
    n1 = norm1_g.reshape(DEPTH, 1, D_MODEL)
    n2 = norm2_g.reshape(DEPTH, 1, D_MODEL)

    for l in range(DEPTH):
        lam_init = 0.8 - 0.6 * math.exp(-0.3 * l)
        for name in ('p', 's'):
            g = groups[name]
            seq, tm, mod_row = g['seq'], g['tm'], g['mod_row']
            x = xs[name]
            lru, qx, k, v, dk, dv = _norm_proj(x, mod, n1, w_in_b, qg, kg, ones_blk, g['tables'],
                                               layer=l, seq=seq, tm=512, mod_row=mod_row)
            h0, state_layer = (zero_state, 0) if name == 'p' else (state_lru, l)
            lru_o, st = _lru(lru, lru_conv_w, lru_cb, gw, gb, lru_lambda, h0,
                             layer=l, state_layer=state_layer, seq=seq, tr=min(seq, 512))
            att_o = _attention(qx, k, v, dk, dv, g['caches'], diff_lambda, dg,
                               layer=l, seq=seq, nseq=g['nseq'], tq=g['tq'], lam_init=lam_init)
            xs[name] = _mix_ffn(lru_o, att_o, x, w_out_b, n2, up_b, ffn_conv_w, ffn_cb, wd_b, mod, fin,
                                layer=l, seq=seq, tm=tm, mod_row=mod_row, final=(l == DEPTH - 1))
            if name == 'p':
                new_k.append(k)
                new_v.append(v)
                new_dk.append(dk)
                new_dv.append(dv)
                new_st.append(st)

    y_prompt = xs['p'].reshape(bp, sp, D_MODEL)
    y_sample = xs['s'].reshape(bs, ss, D_MODEL)
    stack = lambda parts, shape: jnp.stack([p.reshape((bp,) + shape) for p in parts], axis=1)
    return (y_prompt, y_sample,
            stack(new_k, (sp, GQA_KV_HEADS, HEAD_DIM)),
            stack(new_v, (sp, GQA_KV_HEADS, HEAD_DIM)),
            stack(new_dk, (sp, DIFF_HEADS, 2, DIFF_QK_DIM)),
            stack(new_dv, (sp, DIFF_HEADS, HEAD_DIM)),
            jnp.stack(new_st, axis=1))
```

```python
import functools
import math

import numpy as np
import jax
import jax.numpy as jnp
from jax import lax
from jax.experimental import pallas as pl
from jax.experimental.pallas import tpu as pltpu

F32 = jnp.float32
BF16 = jnp.bfloat16

D_MODEL = 1024
DEPTH = 2
GRID_W = 64
HEAD_DIM = 64
LRU_WIDTH = 256
LRU_BLOCKS = 4
LRU_BLOCK_W = LRU_WIDTH // LRU_BLOCKS
LRU_C = 8.0
GQA_HEADS = 8
GQA_KV_HEADS = 2
DIFF_HEADS = 4
DIFF_QK_DIM = 32
D_FF = 2816
ROPE_THETA = 10000.0
EPS = 1e-6

LRU_IN = 2 * LRU_WIDTH
GQA_WIDTH = GQA_HEADS * HEAD_DIM
KV_WIDTH = GQA_KV_HEADS * HEAD_DIM
DIFF_WIDTH = DIFF_HEADS * HEAD_DIM
Q_OFF = LRU_IN
KV_OFF = Q_OFF + GQA_WIDTH
IN_WIDTH = KV_OFF + 2 * KV_WIDTH + 3 * DIFF_WIDTH
LOG2_E = math.log2(math.e)
N_JOBS = GQA_HEADS + 2 * DIFF_HEADS
KEY_CHUNK = 256
ATTN_WIDTH = 512
LANES = 128
FFN_CHUNK = 256
N_FFN_CHUNKS = D_FF // FFN_CHUNK
COND_ROWS = 8
VMEM_LIMIT = 56 * 2 ** 20


def _params(*sem):
    return pltpu.CompilerParams(dimension_semantics=sem, vmem_limit_bytes=VMEM_LIMIT)


def _rms(x):
    return x * lax.rsqrt(jnp.mean(x * x, axis=-1, keepdims=True) + EPS)


def _ada_kernel(cond_ref, w_ref, b_ref, o_ref):
    c = cond_ref[...]
    s = c * jax.nn.sigmoid(c)
    o_ref[...] = jnp.dot(s.astype(BF16), w_ref[...].astype(BF16), preferred_element_type=F32) + b_ref[...]


def _ada(cond, ada_w, ada_b):
    tn = 1536
    width = 6 * D_MODEL
    return pl.pallas_call(
        _ada_kernel,
        grid=(DEPTH, width // tn),
        in_specs=[pl.BlockSpec((COND_ROWS, D_MODEL), lambda l, j: (0, 0)),
                  pl.BlockSpec((None, D_MODEL, tn), lambda l, j: (l, 0, j)),
                  pl.BlockSpec((None, 1, tn), lambda l, j: (l, 0, j))],
        out_specs=pl.BlockSpec((None, COND_ROWS, tn), lambda l, j: (l, 0, j)),
        out_shape=jax.ShapeDtypeStruct((DEPTH, COND_ROWS, width), F32),
        compiler_params=_params("arbitrary", "arbitrary"),
        name="ada",
    )(cond, ada_w, ada_b.reshape(DEPTH, 1, width))


def _group_mean_sq(x, ones_blk):
    sq = x * x
    hi = sq.astype(BF16)
    lo = (sq - hi.astype(F32)).astype(BF16)
    s = jnp.dot(hi, ones_blk, preferred_element_type=F32) + jnp.dot(lo, ones_blk, preferred_element_type=F32)
    return s * (1.0 / HEAD_DIM)


def _rope(x, cos, sin_neg, sin_pos, quarter):
    return x * cos + pltpu.roll(x, LANES - quarter, 1) * sin_neg + pltpu.roll(x, quarter, 1) * sin_pos


def _norm_proj_kernel(*refs, rope):
    if rope:
        (x_ref, mod_ref, n1_ref, w_ref, qg_ref, kg_ref, ones_ref, cq, snq, spq, cd, snd, spd,
         lru_ref, qx_ref, k_ref, v_ref, dk_ref, dv_ref) = refs
    else:
        (x_ref, mod_ref, n1_ref, w_ref, qg_ref, kg_ref, ones_ref,
         lru_ref, qx_ref, k_ref, v_ref, dk_ref, dv_ref) = refs
    mod = mod_ref[...]
    h = _rms(x_ref[...]) * (n1_ref[...] * (1.0 + mod[:, D_MODEL:2 * D_MODEL])) + mod[:, 0:D_MODEL]
    hb = h.astype(BF16)
    ones_blk = ones_ref[...]
    lane = lax.broadcasted_iota(jnp.int32, (1, LANES), 1)

    proj = jnp.dot(hb, w_ref[:, Q_OFF:KV_OFF], preferred_element_type=F32)
    for c in range(4):
        xc = proj[:, LANES * c:LANES * (c + 1)]
        xc = xc * lax.rsqrt(_group_mean_sq(xc, ones_blk) + EPS) * qg_ref[...]
        if rope:
            xc = _rope(xc, cq[...], snq[...], spq[...], HEAD_DIM // 4)
        xc = xc * (HEAD_DIM ** -0.5 * LOG2_E)
        xr = pltpu.roll(xc, HEAD_DIM, 1)
        for par in range(2):
            j = 2 * c + par
            want = j // (GQA_HEADS // GQA_KV_HEADS)
            src = xc if par == want else xr
            qx_ref[j] = jnp.where(lane // HEAD_DIM == want, src, 0.0).astype(BF16)

    proj = jnp.dot(hb, w_ref[:, KV_OFF:IN_WIDTH], preferred_element_type=F32)
    dq_off = 2 * KV_WIDTH
    dk_off = dq_off + DIFF_WIDTH
    dv_off = dk_off + DIFF_WIDTH
    kc = proj[:, 0:KV_WIDTH]
    kc = kc * lax.rsqrt(_group_mean_sq(kc, ones_blk) + EPS) * kg_ref[...]
    if rope:
        kc = _rope(kc, cq[...], snq[...], spq[...], HEAD_DIM // 4)
    k_ref[...] = kc
    v_ref[...] = proj[:, KV_WIDTH:2 * KV_WIDTH]

    for c in range(2):
        xc = proj[:, dq_off + LANES * c:dq_off + LANES * (c + 1)]
        if rope:
            xc = _rope(xc, cd[...], snd[...], spd[...], DIFF_QK_DIM // 4)
        xc = xc * (DIFF_QK_DIM ** -0.5 * LOG2_E)
        for g in range(4):
            qx_ref[GQA_HEADS + 4 * c + g] = jnp.where(lane // DIFF_QK_DIM == g, xc, 0.0).astype(BF16)
        kc = proj[:, dk_off + LANES * c:dk_off + LANES * (c + 1)]
        if rope:
            kc = _rope(kc, cd[...], snd[...], spd[...], DIFF_QK_DIM // 4)
        dk_ref[:, LANES * c:LANES * (c + 1)] = kc
    dv_ref[...] = proj[:, dv_off:dv_off + DIFF_WIDTH]

    lru_ref[...] = jnp.dot(hb, w_ref[:, 0:LRU_IN], preferred_element_type=F32)


def _norm_proj(x, mod, n1, w_in, qg, kg, ones_blk, tables, *, layer, seq, tm, mod_row):
    n = x.shape[0]
    rope = tables is not None
    nts = seq // tm
    row = lambda i: (i, 0)
    const = lambda i: (0, 0)
    per_layer = lambda i: (layer, 0, 0)
    in_specs = [pl.BlockSpec((tm, D_MODEL), row),
                pl.BlockSpec((None, 1, 6 * D_MODEL), lambda i: (layer * COND_ROWS + mod_row(i * tm // seq), 0, 0)),
                pl.BlockSpec((None, 1, D_MODEL), per_layer),
                pl.BlockSpec((None, D_MODEL, IN_WIDTH), per_layer),
                pl.BlockSpec((None, 1, LANES), per_layer),
                pl.BlockSpec((None, 1, LANES), per_layer),
                pl.BlockSpec((LANES, LANES), const)]
    args = [x, mod, n1, w_in, qg, kg, ones_blk]
    if rope:
        in_specs += [pl.BlockSpec((tm, LANES), lambda i: (i % nts, 0))] * 6
        args += list(tables)
    slab = lambda i: (0, i, 0)
    widths = (LRU_IN, KV_WIDTH, KV_WIDTH, DIFF_WIDTH, DIFF_WIDTH)
    out_specs = [pl.BlockSpec((tm, wd), row) for wd in widths]
    out_shape = [jax.ShapeDtypeStruct((n, wd), F32) for wd in widths]
    out_specs.insert(1, pl.BlockSpec((N_JOBS, tm, LANES), slab))
    out_shape.insert(1, jax.ShapeDtypeStruct((N_JOBS, n, LANES), BF16))
    return pl.pallas_call(
        functools.partial(_norm_proj_kernel, rope=rope),
        grid=(n // tm,), in_specs=in_specs, out_specs=out_specs, out_shape=out_shape,
        compiler_params=_params("arbitrary"), name="norm_proj",
    )(*args)


def _chunk_scan(a, u, row, reverse):
    for d in (1, 2, 4):
        shift = 8 - d if reverse else d
        a_s = pltpu.roll(a, shift, 0)
        u_s = pltpu.roll(u, shift, 0)
        m = (row < 8 - d) if reverse else (row >= d)
        u = jnp.where(m, a * u_s + u, u)
        a = jnp.where(m, a * a_s, a)
    return a, u


def _lru_kernel(x_ref, xp_ref, xn_ref, cw_ref, cb_ref, gw_ref, gb_ref, lam_ref, h0_ref, out_ref, st_ref,
                af, uf, ab, ub, gg, *, seq, tr):
    j = pl.program_id(1)
    nt = seq // tr
    xg = x_ref[...]
    x = xg[:, 0:LRU_WIDTH]
    prev = jnp.where(j > 0, xp_ref[...], 0.0)
    nxt = jnp.where(j < nt - 1, xn_ref[...], 0.0)
    xe = jnp.concatenate([prev, x, nxt], axis=0)
    ne = tr + 16
    cw = cw_ref[...]
    xc = (cw[0:1] * pltpu.roll(xe, 2, 0)[8:8 + tr] + cw[1:2] * pltpu.roll(xe, 1, 0)[8:8 + tr]
          + cw[2:3] * x + cw[3:4] * pltpu.roll(xe, ne - 1, 0)[8:8 + tr] + cb_ref[...])
    sg = 0.5 * jnp.tanh(jnp.dot(xc.astype(BF16), gw_ref[...], preferred_element_type=F32) + gb_ref[...]) + 0.5
    z = -lam_ref[...]
    neg_c_softplus = -LRU_C * (jnp.maximum(z, 0.0) + jnp.log1p(jnp.exp(-jnp.abs(z))))
    r0 = pl.multiple_of(j * tr, tr)
    for d, (a_s, u_s) in enumerate(((af, uf), (ab, ub))):
        r = sg[:, 2 * LRU_WIDTH * d:2 * LRU_WIDTH * d + LRU_WIDTH]
        i = sg[:, 2 * LRU_WIDTH * d + LRU_WIDTH:2 * LRU_WIDTH * (d + 1)]
        log_a = neg_c_softplus[d:d + 1] * r
        a = jnp.exp(log_a)
        a_s[pl.ds(r0, tr), :] = a
        u_s[pl.ds(r0, tr), :] = jnp.sqrt(-jnp.tanh(log_a) * (a * a + 1.0)) * (i * xc)
    gg[pl.ds(r0, tr), :] = jax.nn.gelu(xg[:, LRU_WIDTH:2 * LRU_WIDTH])

    @pl.when(j == nt - 1)
    def _():
        row = lax.broadcasted_iota(jnp.int32, (8, LRU_WIDTH), 0)
        nchunk = seq // 8

        def body(c, carry):
            hf, hb = carry
            rf = pl.multiple_of(c * 8, 8)
            a, u = _chunk_scan(af[pl.ds(rf, 8), :], uf[pl.ds(rf, 8), :], row, False)
            hs = u + a * hf
            uf[pl.ds(rf, 8), :] = hs
            rb = pl.multiple_of((nchunk - 1 - c) * 8, 8)
            a2, u2 = _chunk_scan(ab[pl.ds(rb, 8), :], ub[pl.ds(rb, 8), :], row, True)
            hs2 = u2 + a2 * hb
            ub[pl.ds(rb, 8), :] = hs2
            return hs[7:8, :], hs2[0:1, :]

        h0 = h0_ref[...]
        hf, hb = lax.fori_loop(0, nchunk, body, (h0[0:1], h0[1:2]), unroll=4)
        st_ref[...] = jnp.concatenate([hf, hb], axis=0)

        def obody(t, carry):
            r = pl.multiple_of(t * tr, tr)
            out_ref[pl.ds(r, tr), :] = (gg[pl.ds(r, tr), :] * (uf[pl.ds(r, tr), :] + ub[pl.ds(r, tr), :])).astype(BF16)
            return carry

        lax.fori_loop(0, nt, obody, 0)


def _lru(lru, cw, cb, gw, gb, lam, h0, *, layer, state_layer, seq, tr):
    n = lru.shape[0]
    b = n // seq
    nt = seq // tr
    nb8 = n // 8
    per_layer = lambda bi, j: (layer, 0, 0)
    in_specs = [pl.BlockSpec((tr, LRU_IN), lambda bi, j: (bi * nt + j, 0)),
                pl.BlockSpec((8, LRU_WIDTH), lambda bi, j: (jnp.maximum((bi * nt + j) * (tr // 8) - 1, 0), 0)),
                pl.BlockSpec((8, LRU_WIDTH), lambda bi, j: (jnp.minimum((bi * nt + j + 1) * (tr // 8), nb8 - 1), 0)),
                pl.BlockSpec((None, 4, LRU_WIDTH), per_layer),
                pl.BlockSpec((None, 1, LRU_WIDTH), per_layer),
                pl.BlockSpec((None, LRU_WIDTH, 4 * LRU_WIDTH), per_layer),
                pl.BlockSpec((None, 1, 4 * LRU_WIDTH), per_layer),
                pl.BlockSpec((None, 2, LRU_WIDTH), per_layer),
                pl.BlockSpec((None, None, 2, LRU_WIDTH), lambda bi, j: (bi, state_layer, 0, 0))]
    out_specs = [pl.BlockSpec((seq, LRU_WIDTH), lambda bi, j: (bi, 0)),
                 pl.BlockSpec((None, 2, LRU_WIDTH), lambda bi, j: (bi, 0, 0))]
    out_shape = [jax.ShapeDtypeStruct((n, LRU_WIDTH), BF16),
                 jax.ShapeDtypeStruct((b, 2, LRU_WIDTH), F32)]
    return pl.pallas_call(
        functools.partial(_lru_kernel, seq=seq, tr=tr),
        grid=(b, nt), in_specs=in_specs, out_specs=out_specs, out_shape=out_shape,
        scratch_shapes=[pltpu.VMEM((seq, LRU_WIDTH), F32)] * 5,
        compiler_params=_params("arbitrary", "arbitrary"), name="lru",
    )(lru, lru, lru, cw, cb, gw, gb, lam, h0)


_NT = (((1,), (1,)), ((), ()))


def _attn_kernel(*refs, n_ctx, seq, nseq, tq, pack, ways, lam_init):
    n_in = 11 if n_ctx else 7
    if n_ctx:
        q_ref, k_ref, v_ref, dk_ref, dv_ref, ck_ref, cv_ref, cdk_ref, cdv_ref, dl_ref, dg_ref = refs[:n_in]
    else:
        q_ref, k_ref, v_ref, dk_ref, dv_ref, dl_ref, dg_ref = refs[:n_in]
    out_ref, ks, vts = refs[n_in:n_in + 3]
    s_bufs = refs[n_in + 3:n_in + 3 + ways]
    e_bufs = refs[n_in + 3 + ways:n_in + 3 + 2 * ways]
    o_buf = refs[n_in + 3 + 2 * ways]
    t_all = n_ctx + seq
    n_chunks = t_all // KEY_CHUNK

    def put(slab, new, ctx):
        if n_ctx:
            ks[slab, 0:n_ctx, :] = ctx[0].astype(BF16)
            vts[slab, :, 0:n_ctx] = ctx[1].astype(BF16).T
        ks[slab, n_ctx:t_all, :] = new[0].astype(BF16)
        vts[slab, :, n_ctx:t_all] = new[1].astype(BF16).T

    for s in range(nseq):
        rs = slice(seq * s, seq * (s + 1))
        put(3 * s, (k_ref[rs, :], v_ref[rs, :]), (ck_ref[...], cv_ref[...]) if n_ctx else None)
        for c in range(2):
            cols = slice(LANES * c, LANES * (c + 1))
            put(3 * s + 1 + c, (dk_ref[rs, cols], dv_ref[rs, cols]),
                (cdk_ref[:, cols], cdv_ref[:, cols]) if n_ctx else None)

    width = pack * tq
    upb = N_JOBS // pack
    blocks_per_seq = seq // tq
    n_units = upb * blocks_per_seq * nseq

    def slab_of(t):
        g = (t % upb) * pack
        return 3 * (t // (upb * blocks_per_seq)) + jnp.where(g < GQA_HEADS, 0, 1 + (g - GQA_HEADS) // 4)

    def q_unit(t):
        u, qb = t % upb, t // upb
        r0 = qb * tq if isinstance(t, int) else pl.multiple_of(qb * tq, tq)
        if pack == 1:
            return q_ref[u, pl.ds(r0, tq), :]
        return q_ref[pl.ds(u * pack, pack), pl.ds(r0, tq), :].reshape(width, LANES)

    def fold8(x, op):
        acc = x[0:8]
        for r in range(1, KEY_CHUNK // 8):
            acc = op(acc, x[8 * r:8 * (r + 1)])
        return acc

    def run(t, par, m_prev, l_prev, do_a=True, do_b=True, do_c=True):
        ga, gb, gc = t, t - ways, t - 2 * ways
        s_buf, e_buf = s_bufs[par], e_bufs[par]
        m_acc = l_acc = None
        if do_a:
            qa = q_unit(ga)
            ka = slab_of(ga)
            m_acc = jnp.full((8, width), -jnp.inf, F32)
        if do_b:
            m_row = jnp.max(m_prev, axis=0, keepdims=True)
            l_acc = jnp.zeros((8, width), F32)
        if do_c:
            vc = slab_of(gc)
            o_acc = jnp.zeros((LANES, width), F32)
        for c in range(n_chunks):
            rows = slice(KEY_CHUNK * c, KEY_CHUNK * (c + 1))
            if do_c:
                o_acc = o_acc + jnp.dot(vts[vc, :, rows], e_buf[rows, :], preferred_element_type=F32)
            if do_b:
                e = jnp.exp2(s_buf[rows, :] - m_row)
                e_buf[rows, :] = e.astype(BF16)
                l_acc = l_acc + fold8(e, jnp.add)
            if do_a:
                s = lax.dot_general(ks[ka, rows, :], qa, _NT, preferred_element_type=F32)
                s_buf[rows, :] = s
                m_acc = jnp.maximum(m_acc, fold8(s, jnp.maximum))
        if do_c:
            o_buf[gc % upb] = o_acc * (1.0 / jnp.sum(l_prev, axis=0, keepdims=True))
        return m_acc, l_acc

    def job_out(g, rows):
        return o_buf[g // pack, rows, tq * (g % pack):tq * (g % pack + 1)]

    def finish_block(qb):
        r0 = qb * tq if isinstance(qb, int) else pl.multiple_of(qb * tq, tq)
        out_rows = pl.ds(r0, tq)
        for c in range(4):
            h = c // 2
            rows = slice(HEAD_DIM * h, HEAD_DIM * (h + 1))
            ot = jnp.concatenate([job_out(2 * c, rows), job_out(2 * c + 1, rows)], axis=0)
            out_ref[out_rows, LANES * c:LANES * (c + 1)] = ot.astype(BF16).T

        dl = dl_ref[...]
        lam = (jnp.exp(jnp.sum(dl[0:1] * dl[1:2], axis=-1, keepdims=True))
               - jnp.exp(jnp.sum(dl[2:3] * dl[3:4], axis=-1, keepdims=True)) + lam_init)

        gain = dg_ref[...] * (1.0 - lam_init)
        for c in range(2):
            parts = []
            for par in range(2):
                g1 = GQA_HEADS + 2 * (2 * c + par)
                rows = slice(HEAD_DIM * par, HEAD_DIM * (par + 1))
                o = job_out(g1, rows) - lam * job_out(g1 + 1, rows)
                parts.append(o * lax.rsqrt(jnp.mean(o * o, axis=0, keepdims=True) + EPS))
            ot = jnp.concatenate(parts, axis=0)
            ot = jnp.concatenate([ot[:, LANES * j:LANES * (j + 1)] * gain for j in range(tq // LANES)], axis=1)
            out_ref[out_rows, GQA_WIDTH + LANES * c:GQA_WIDTH + LANES * (c + 1)] = ot.astype(BF16).T

    bpb = upb // ways
    n_bodies = n_units // ways
    nothing = (None,) * ways

    def body(i, carry, **stages):
        ms, ls = [], []
        for par in range(ways):
            m, l = run(ways * i + par, par, carry[0][par], carry[1][par], **stages)
            ms.append(m)
            ls.append(l)
        if isinstance(i, int):
            if i > 1 and (i - 1) % bpb == 0:
                finish_block((i - 1) // bpb - 1)
        elif n_bodies > bpb:
            @pl.when((i - 1) % bpb == 0)
            def _():
                finish_block((i - 1) // bpb - 1)
        return tuple(ms), tuple(ls)

    carry = body(0, (nothing, nothing), do_b=False, do_c=False)
    carry = body(1, (carry[0], nothing), do_c=False)
    carry = lax.fori_loop(2, n_bodies, body, carry)
    carry = body(n_bodies, carry, do_a=False)
    body(n_bodies + 1, (nothing, carry[1]), do_a=False, do_b=False)


def _attention(qx, k, v, dk, dv, caches, dl, dg, *, layer, seq, nseq, tq, lam_init):
    pack = ATTN_WIDTH // tq
    ways = 2 if seq > ATTN_WIDTH else 4
    n = k.shape[0]
    rows = nseq * seq
    n_ctx = 0 if caches is None else caches[0].shape[2]
    assert n_ctx == 0 or nseq == 1
    t_all = n_ctx + seq
    per_b = lambda bi: (bi, 0)
    once = pl.Buffered(1) if seq > ATTN_WIDTH else None
    in_specs = [pl.BlockSpec((N_JOBS, rows, LANES), lambda bi: (0, bi, 0)),
                pl.BlockSpec((rows, KV_WIDTH), per_b, pipeline_mode=once),
                pl.BlockSpec((rows, KV_WIDTH), per_b, pipeline_mode=once),
                pl.BlockSpec((rows, DIFF_WIDTH), per_b, pipeline_mode=once),
                pl.BlockSpec((rows, DIFF_WIDTH), per_b, pipeline_mode=once)]
    args = [qx, k, v, dk, dv]
    if n_ctx:
        cache_idx = lambda bi: (bi, layer, 0, 0)
        in_specs += [pl.BlockSpec((None, None, n_ctx, KV_WIDTH), cache_idx),
                     pl.BlockSpec((None, None, n_ctx, KV_WIDTH), cache_idx),
                     pl.BlockSpec((None, None, n_ctx, DIFF_WIDTH), cache_idx),
                     pl.BlockSpec((None, None, n_ctx, DIFF_WIDTH), cache_idx)]
        args += list(caches)
    in_specs += [pl.BlockSpec((None, 4, DIFF_QK_DIM), lambda bi: (layer, 0, 0)),
                 pl.BlockSpec((None, LANES, LANES), lambda bi: (layer, 0, 0))]
    args += [dl, dg]
    return pl.pallas_call(
        functools.partial(_attn_kernel, n_ctx=n_ctx, seq=seq, nseq=nseq, tq=tq, pack=pack, ways=ways,
                          lam_init=lam_init),
        grid=(n // rows,), in_specs=in_specs,
        out_specs=pl.BlockSpec((rows, D_MODEL - LRU_WIDTH), per_b),
        out_shape=jax.ShapeDtypeStruct((n, D_MODEL - LRU_WIDTH), BF16),
        scratch_shapes=[pltpu.VMEM((3 * nseq, t_all, LANES), BF16), pltpu.VMEM((3 * nseq, LANES, t_all), BF16)]
        + [pltpu.VMEM((t_all, ATTN_WIDTH), F32)] * ways + [pltpu.VMEM((t_all, ATTN_WIDTH), BF16)] * ways
        + [pltpu.VMEM((N_JOBS // pack, LANES, ATTN_WIDTH), F32)],
        compiler_params=_params("arbitrary"), name="attention",
    )(*args)


HALO = 16


def _zero_row(x, r):
    g = r // 8 * 8
    row = lax.broadcasted_iota(jnp.int32, (8, x.shape[1]), 0)
    patched = jnp.where(row == r - g, 0.0, x[g:g + 8])
    return jnp.concatenate([x[:g], patched, x[g + 8:]], axis=0)


def _mix_ffn_kernel(lru_ref, lrup_ref, lrun_ref, att_ref, attp_ref, attn_ref, x_ref, xp_ref, xn_ref,
                    wo_ref, n2_ref, up_ref, cw_ref, cb_ref, wd_ref, mod_ref, fin_ref, o_ref,
                    mix, xe, he, act, *, seq, tm, final):
    i = pl.program_id(0)
    nts = max(seq // tm, 1)
    pos = i % nts
    ne = tm + 2 * HALO
    for r0, nr, lru_r, att_r, x_r in ((0, HALO, lrup_ref, attp_ref, xp_ref),
                                      (HALO, tm, lru_ref, att_ref, x_ref),
                                      (HALO + tm, HALO, lrun_ref, attn_ref, xn_ref)):
        mix[r0:r0 + nr, 0:LRU_WIDTH] = lru_r[...]
        mix[r0:r0 + nr, LRU_WIDTH:D_MODEL] = att_r[...]
        xe[r0:r0 + nr, :] = x_r[...]
    mod = mod_ref[...]
    x1 = xe[...] + mod[:, 2 * D_MODEL:3 * D_MODEL] * jnp.dot(mix[...], wo_ref[...], preferred_element_type=F32)
    xe[...] = x1
    he[...] = (_rms(x1) * (n2_ref[...] * (1.0 + mod[:, 4 * D_MODEL:5 * D_MODEL]))
               + mod[:, 3 * D_MODEL:4 * D_MODEL]).astype(BF16)

    @pl.when(pos == 0)
    def _():
        he[0:HALO, :] = jnp.zeros((HALO, D_MODEL), BF16)

    @pl.when(pos == nts - 1)
    def _():
        he[HALO + tm:ne, :] = jnp.zeros((HALO, D_MODEL), BF16)

    def up(c, half, scale):
        cols = slice(half * D_FF + FFN_CHUNK * c, half * D_FF + FFN_CHUNK * (c + 1))
        u = jnp.dot(he[...], up_ref[:, cols], preferred_element_type=F32)
        cw, cb = cw_ref[:, cols], cb_ref[:, cols]
        if scale != 1.0:
            cw, cb = cw * scale, cb * scale
        before = pltpu.roll(u, 1, 0)[HALO:HALO + tm]
        after = pltpu.roll(u, ne - 1, 0)[HALO:HALO + tm]
        for b in range(seq, tm, seq):
            before = _zero_row(before, b)
            after = _zero_row(after, b - 1)
        return cw[0:1] * before + cw[1:2] * u[HALO:HALO + tm] + cw[2:3] * after + cb

    for c in range(N_FFN_CHUNKS):
        a, hg = up(c, 0, 1.0), up(c, 1, 0.5)
        act[:, FFN_CHUNK * c:FFN_CHUNK * (c + 1)] = ((hg * jnp.tanh(hg) + hg) * a).astype(BF16)
    down = jnp.dot(act[...], wd_ref[...], preferred_element_type=F32)
    x2 = xe[HALO:HALO + tm, :] + mod[:, 5 * D_MODEL:6 * D_MODEL] * down
    if final:
        x2 = _rms(x2) * fin_ref[...]
    o_ref[...] = x2


def _mix_ffn(lru_o, att_o, x, w_out, n2, up, cw, cb, wd, mod, fin_g, *, layer, seq, tm, mod_row, final):
    n = x.shape[0]
    assert seq % tm == 0 or tm % seq == 0
    nts = max(seq // tm, 1)
    nbh = n // HALO
    row = lambda i: (i, 0)
    prev = lambda i: (jnp.maximum(i * (tm // HALO) - 1, 0), 0)
    nxt = lambda i: (jnp.minimum((i + 1) * (tm // HALO), nbh - 1), 0)
    c2 = lambda i: (0, 0)
    per_layer = lambda i: (layer, 0, 0)
    att_w = D_MODEL - LRU_WIDTH

    def with_halo(width):
        return [pl.BlockSpec((tm, width), row), pl.BlockSpec((HALO, width), prev), pl.BlockSpec((HALO, width), nxt)]

    in_specs = (with_halo(LRU_WIDTH) + with_halo(att_w) + with_halo(D_MODEL)
                + [pl.BlockSpec((None, D_MODEL, D_MODEL), per_layer),
                   pl.BlockSpec((None, 1, D_MODEL), per_layer),
                   pl.BlockSpec((None, D_MODEL, 2 * D_FF), per_layer),
                   pl.BlockSpec((None, 3, 2 * D_FF), per_layer),
                   pl.BlockSpec((None, 1, 2 * D_FF), per_layer),
                   pl.BlockSpec((None, D_FF, D_MODEL), per_layer),
                   pl.BlockSpec((None, 1, 6 * D_MODEL), lambda i: (layer * COND_ROWS + mod_row(i // nts), 0, 0)),
                   pl.BlockSpec((1, D_MODEL), c2)])
    return pl.pallas_call(
        functools.partial(_mix_ffn_kernel, seq=seq, tm=tm, final=final),
        grid=(n // tm,), in_specs=in_specs,
        out_specs=pl.BlockSpec((tm, D_MODEL), row),
        out_shape=jax.ShapeDtypeStruct((n, D_MODEL), F32),
        scratch_shapes=[pltpu.VMEM((tm + 2 * HALO, D_MODEL), BF16), pltpu.VMEM((tm + 2 * HALO, D_MODEL), F32),
                        pltpu.VMEM((tm + 2 * HALO, D_MODEL), BF16), pltpu.VMEM((tm, D_FF), BF16)],
        compiler_params=_params("arbitrary"), name="mix_ffn",
    )(lru_o, lru_o, lru_o, att_o, att_o, att_o, x, x, x, w_out, n2, up, cw, cb, wd, mod, fin_g)


def _rope_tables(seq, dim):
    rows = seq // GRID_W
    t_row = np.repeat(np.arange(rows, dtype=np.float64), GRID_W)
    t_col = np.tile(np.arange(GRID_W, dtype=np.float64), rows)
    axis_dim = dim // 2
    inv = ROPE_THETA ** (-np.arange(0, axis_dim, 2, dtype=np.float64) / axis_dim)
    ar = t_row[:, None] * inv
    ac = t_col[:, None] * inv
    ang = np.concatenate([ar, ar, ac, ac], axis=-1)
    reps = LANES // dim
    cos = np.tile(np.cos(ang), (1, reps))
    sin = np.tile(np.sin(ang), (1, reps))
    first = (np.arange(LANES) % (dim // 2)) < (dim // 4)
    sin_neg = np.where(first, -sin, 0.0)
    sin_pos = np.where(first, 0.0, sin)
    return tuple(jnp.asarray(t, F32) for t in (cos, sin_neg, sin_pos))


def _gate_weights(w, b):
    nl = w.shape[0]
    eye = jnp.eye(LRU_BLOCKS, dtype=w.dtype)
    wp = w.reshape(nl, 2, LRU_BLOCKS, LRU_BLOCK_W, 2, LRU_BLOCK_W)
    dense = jnp.einsum('lpndqe,nm->lndpqme', wp, eye).reshape(nl, LRU_WIDTH, 4 * LRU_WIDTH)
    bias = jnp.transpose(b.reshape(nl, 2, LRU_BLOCKS, 2, LRU_BLOCK_W), (0, 1, 3, 2, 4)).reshape(nl, 1, 4 * LRU_WIDTH)
    return (0.5 * dense).astype(BF16), 0.5 * bias


def kernel(x_prompt, x_sample, cache_gqa_k, cache_gqa_v, cache_diff_k, cache_diff_v, state_lru, c, c_ctx,
           norm1_g, norm2_g, final_norm_g, ada_w, ada_b, w_in, w_out, lru_conv_w, lru_conv_b, lru_gate_w,
           lru_gate_b, lru_lambda, gqa_q_norm_g, gqa_k_norm_g, diff_lambda, diff_norm_g, ffn_w_up, ffn_conv_w,
           ffn_conv_b, ffn_w_down):
    bp, sp, _ = x_prompt.shape
    bs, ss, _ = x_sample.shape
    n_ctx = cache_gqa_k.shape[2]

    cond = jnp.concatenate([c_ctx[None, :], c, jnp.zeros((COND_ROWS - 1 - bs, D_MODEL), F32)], axis=0)
    mod_all = _ada(cond, ada_w, ada_b)

    caches = (cache_gqa_k.reshape(bs, DEPTH, n_ctx, KV_WIDTH), cache_gqa_v.reshape(bs, DEPTH, n_ctx, KV_WIDTH),
              cache_diff_k.reshape(bs, DEPTH, n_ctx, DIFF_WIDTH), cache_diff_v.reshape(bs, DEPTH, n_ctx, DIFF_WIDTH))
    tables = _rope_tables(ss, HEAD_DIM) + _rope_tables(ss, DIFF_QK_DIM)
    ones_blk = jnp.asarray(np.kron(np.eye(2), np.ones((HEAD_DIM, HEAD_DIM))), BF16)
    zero_state = jnp.zeros((bp, 1, 2, LRU_WIDTH), F32)

    groups = {
        'p': dict(seq=sp, tm=512, tq=256, nseq=4, mod_row=lambda b: 0, tables=None, caches=None),
        's': dict(seq=ss, tm=512, tq=512, nseq=1, mod_row=lambda b: 1 + b, tables=tables, caches=caches),
    }
    xs = {'p': x_prompt.reshape(bp * sp, D_MODEL), 's': x_sample.reshape(bs * ss, D_MODEL)}
    new_k, new_v, new_dk, new_dv, new_st = [], [], [], [], []

    mod = mod_all.reshape(DEPTH * COND_ROWS, 1, 6 * D_MODEL)
    w_in_b = w_in.astype(BF16)
    w_out_b = w_out.astype(BF16)
    up_b = ffn_w_up.astype(BF16)
    wd_b = ffn_w_down.astype(BF16)
    ffn_cb = ffn_conv_b.reshape(DEPTH, 1, 2 * D_FF)
    fin = final_norm_g[None, :]
    gw, gb = _gate_weights(lru_gate_w, lru_gate_b)
    lru_cb = lru_conv_b.reshape(DEPTH, 1, LRU_WIDTH)
    qg = jnp.tile(gqa_q_norm_g, (1, 2)).reshape(DEPTH, 1, LANES)
    kg = jnp.tile(gqa_k_norm_g, (1, 2)).reshape(DEPTH, 1, LANES)
    dg = jnp.broadcast_to(jnp.tile(diff_norm_g, (1, 2))[:, :, None], (DEPTH, LANES, LANES))
    n1 = norm1_g.reshape(DEPTH, 1, D_MODEL)
    n2 = norm2_g.reshape(DEPTH, 1, D_MODEL)

    for l in range(DEPTH):
        lam_init = 0.8 - 0.6 * math.exp(-0.3 * l)
        for name in ('p', 's'):
            g = groups[name]
            seq, tm, mod_row = g['seq'], g['tm'], g['mod_row']
            x = xs[name]
            lru, qx, k, v, dk, dv = _norm_proj(x, mod, n1, w_in_b, qg, kg, ones_blk, g['tables'],
                                               layer=l, seq=seq, tm=1024 if seq % 1024 == 0 else 512,
                                               mod_row=mod_row)
            h0, state_layer = (zero_state, 0) if name == 'p' else (state_lru, l)
            lru_o, st = _lru(lru, lru_conv_w, lru_cb, gw, gb, lru_lambda, h0,
                             layer=l, state_layer=state_layer, seq=seq, tr=min(seq, 512))
            att_o = _attention(qx, k, v, dk, dv, g['caches'], diff_lambda, dg,
                               layer=l, seq=seq, nseq=g['nseq'], tq=g['tq'], lam_init=lam_init)
            xs[name] = _mix_ffn(lru_o, att_o, x, w_out_b, n2, up_b, ffn_conv_w, ffn_cb, wd_b, mod, fin,
                                layer=l, seq=seq, tm=tm, mod_row=mod_row, final=(l == DEPTH - 1))
            if name == 'p':
                new_k.append(k)
                new_v.append(v)
                new_dk.append(dk)
                new_dv.append(dv)
                new_st.append(st)

    y_prompt = xs['p'].reshape(bp, sp, D_MODEL)
    y_sample = xs['s'].reshape(bs, ss, D_MODEL)
    stack = lambda parts, shape: jnp.stack([p.reshape((bp,) + shape) for p in parts], axis=1)
    return (y_prompt, y_sample,
            stack(new_k, (sp, GQA_KV_HEADS, HEAD_DIM)),
            stack(new_v, (sp, GQA_KV_HEADS, HEAD_DIM)),
            stack(new_dk, (sp, DIFF_HEADS, 2, DIFF_QK_DIM)),
            stack(new_dv, (sp, DIFF_HEADS, HEAD_DIM)),
            jnp.stack(new_st, axis=1))
```

```python
import functools
import math

import numpy as np
import jax
import jax.numpy as jnp
from jax import lax
from jax.experimental import pallas as pl
from jax.experimental.pallas import tpu as pltpu

F32 = jnp.float32
BF16 = jnp.bfloat16

D_MODEL = 1024
DEPTH = 2
GRID_W = 64
HEAD_DIM = 64
LRU_WIDTH = 256
LRU_BLOCKS = 4
LRU_BLOCK_W = LRU_WIDTH // LRU_BLOCKS
LRU_C = 8.0
GQA_HEADS = 8
GQA_KV_HEADS = 2
DIFF_HEADS = 4
DIFF_QK_DIM = 32
D_FF = 2816
ROPE_THETA = 10000.0
EPS = 1e-6

LRU_IN = 2 * LRU_WIDTH
GQA_WIDTH = GQA_HEADS * HEAD_DIM
KV_WIDTH = GQA_KV_HEADS * HEAD_DIM
DIFF_WIDTH = DIFF_HEADS * HEAD_DIM
Q_OFF = LRU_IN
KV_OFF = Q_OFF + GQA_WIDTH
IN_WIDTH = KV_OFF + 2 * KV_WIDTH + 3 * DIFF_WIDTH
LOG2_E = math.log2(math.e)
N_JOBS = GQA_HEADS + 2 * DIFF_HEADS
KEY_CHUNK = 256
ATTN_WIDTH = 512
LANES = 128
FFN_CHUNK = 256
N_FFN_CHUNKS = D_FF // FFN_CHUNK
COND_ROWS = 8
VMEM_LIMIT = 56 * 2 ** 20


def _params(*sem):
    return pltpu.CompilerParams(dimension_semantics=sem, vmem_limit_bytes=VMEM_LIMIT)


def _rms(x):
    return x * lax.rsqrt(jnp.mean(x * x, axis=-1, keepdims=True) + EPS)


def _ada_kernel(cond_ref, w_ref, b_ref, o_ref):
    c = cond_ref[...]
    s = c * jax.nn.sigmoid(c)
    o_ref[...] = jnp.dot(s.astype(BF16), w_ref[...].astype(BF16), preferred_element_type=F32) + b_ref[...]


def _ada(cond, ada_w, ada_b):
    tn = 1536
    width = 6 * D_MODEL
    return pl.pallas_call(
        _ada_kernel,
        grid=(DEPTH, width // tn),
        in_specs=[pl.BlockSpec((COND_ROWS, D_MODEL), lambda l, j: (0, 0)),
                  pl.BlockSpec((None, D_MODEL, tn), lambda l, j: (l, 0, j)),
                  pl.BlockSpec((None, 1, tn), lambda l, j: (l, 0, j))],
        out_specs=pl.BlockSpec((None, COND_ROWS, tn), lambda l, j: (l, 0, j)),
        out_shape=jax.ShapeDtypeStruct((DEPTH, COND_ROWS, width), F32),
        compiler_params=_params("arbitrary", "arbitrary"),
        name="ada",
    )(cond, ada_w, ada_b.reshape(DEPTH, 1, width))


def _group_mean_sq(x, ones_blk):
    sq = x * x
    hi = sq.astype(BF16)
    lo = (sq - hi.astype(F32)).astype(BF16)
    s = jnp.dot(hi, ones_blk, preferred_element_type=F32) + jnp.dot(lo, ones_blk, preferred_element_type=F32)
    return s * (1.0 / HEAD_DIM)


def _rope(x, cos, sin_neg, sin_pos, quarter):
    return x * cos + pltpu.roll(x, LANES - quarter, 1) * sin_neg + pltpu.roll(x, quarter, 1) * sin_pos


def _norm_proj_kernel(*refs, rope):
    if rope:
        (x_ref, mod_ref, n1_ref, w_ref, qg_ref, kg_ref, ones_ref, cq, snq, spq, cd, snd, spd,
         lru_ref, qx_ref, k_ref, v_ref, dk_ref, dv_ref) = refs
    else:
        (x_ref, mod_ref, n1_ref, w_ref, qg_ref, kg_ref, ones_ref,
         lru_ref, qx_ref, k_ref, v_ref, dk_ref, dv_ref) = refs
    mod = mod_ref[...]
    h = _rms(x_ref[...]) * (n1_ref[...] * (1.0 + mod[:, D_MODEL:2 * D_MODEL])) + mod[:, 0:D_MODEL]
    hb = h.astype(BF16)
    ones_blk = ones_ref[...]
    lane = lax.broadcasted_iota(jnp.int32, (1, LANES), 1)

    proj = jnp.dot(hb, w_ref[:, Q_OFF:KV_OFF], preferred_element_type=F32)
    for c in range(4):
        xc = proj[:, LANES * c:LANES * (c + 1)]
        xc = xc * lax.rsqrt(_group_mean_sq(xc, ones_blk) + EPS) * qg_ref[...]
        if rope:
            xc = _rope(xc, cq[...], snq[...], spq[...], HEAD_DIM // 4)
        xc = xc * (HEAD_DIM ** -0.5 * LOG2_E)
        xr = pltpu.roll(xc, HEAD_DIM, 1)
        for par in range(2):
            j = 2 * c + par
            want = j // (GQA_HEADS // GQA_KV_HEADS)
            src = xc if par == want else xr
            qx_ref[j] = jnp.where(lane // HEAD_DIM == want, src, 0.0).astype(BF16)

    proj = jnp.dot(hb, w_ref[:, KV_OFF:IN_WIDTH], preferred_element_type=F32)
    dq_off = 2 * KV_WIDTH
    dk_off = dq_off + DIFF_WIDTH
    dv_off = dk_off + DIFF_WIDTH
    kc = proj[:, 0:KV_WIDTH]
    kc = kc * lax.rsqrt(_group_mean_sq(kc, ones_blk) + EPS) * kg_ref[...]
    if rope:
        kc = _rope(kc, cq[...], snq[...], spq[...], HEAD_DIM // 4)
    k_ref[...] = kc.astype(k_ref.dtype)
    v_ref[...] = proj[:, KV_WIDTH:2 * KV_WIDTH].astype(v_ref.dtype)

    for c in range(2):
        xc = proj[:, dq_off + LANES * c:dq_off + LANES * (c + 1)]
        if rope:
            xc = _rope(xc, cd[...], snd[...], spd[...], DIFF_QK_DIM // 4)
        xc = xc * (DIFF_QK_DIM ** -0.5 * LOG2_E)
        for g in range(4):
            qx_ref[GQA_HEADS + 4 * c + g] = jnp.where(lane // DIFF_QK_DIM == g, xc, 0.0).astype(BF16)
        kc = proj[:, dk_off + LANES * c:dk_off + LANES * (c + 1)]
        if rope:
            kc = _rope(kc, cd[...], snd[...], spd[...], DIFF_QK_DIM // 4)
        dk_ref[:, LANES * c:LANES * (c + 1)] = kc.astype(dk_ref.dtype)
    dv_ref[...] = proj[:, dv_off:dv_off + DIFF_WIDTH].astype(dv_ref.dtype)

    lru_ref[...] = jnp.dot(hb, w_ref[:, 0:LRU_IN], preferred_element_type=F32)


def _norm_proj(x, mod, n1, w_in, qg, kg, ones_blk, tables, *, layer, seq, tm, mod_row, kv_dtype):
    n = x.shape[0]
    rope = tables is not None
    nts = seq // tm
    row = lambda i: (i, 0)
    const = lambda i: (0, 0)
    per_layer = lambda i: (layer, 0, 0)
    in_specs = [pl.BlockSpec((tm, D_MODEL), row),
                pl.BlockSpec((None, 1, 6 * D_MODEL), lambda i: (layer * COND_ROWS + mod_row(i * tm // seq), 0, 0)),
                pl.BlockSpec((None, 1, D_MODEL), per_layer),
                pl.BlockSpec((None, D_MODEL, IN_WIDTH), per_layer),
                pl.BlockSpec((None, 1, LANES), per_layer),
                pl.BlockSpec((None, 1, LANES), per_layer),
                pl.BlockSpec((LANES, LANES), const)]
    args = [x, mod, n1, w_in, qg, kg, ones_blk]
    if rope:
        in_specs += [pl.BlockSpec((tm, LANES), lambda i: (i % nts, 0))] * 6
        args += list(tables)
    slab = lambda i: (0, i, 0)
    widths = (LRU_IN, KV_WIDTH, KV_WIDTH, DIFF_WIDTH, DIFF_WIDTH)
    out_specs = [pl.BlockSpec((tm, wd), row) for wd in widths]
    out_shape = [jax.ShapeDtypeStruct((n, wd), F32 if j == 0 else kv_dtype) for j, wd in enumerate(widths)]
    out_specs.insert(1, pl.BlockSpec((N_JOBS, tm, LANES), slab))
    out_shape.insert(1, jax.ShapeDtypeStruct((N_JOBS, n, LANES), BF16))
    return pl.pallas_call(
        functools.partial(_norm_proj_kernel, rope=rope),
        grid=(n // tm,), in_specs=in_specs, out_specs=out_specs, out_shape=out_shape,
        compiler_params=_params("arbitrary"), name="norm_proj",
    )(*args)


def _chunk_scan(a, u, row, reverse):
    for d in (1, 2, 4):
        shift = 8 - d if reverse else d
        a_s = pltpu.roll(a, shift, 0)
        u_s = pltpu.roll(u, shift, 0)
        m = (row < 8 - d) if reverse else (row >= d)
        u = jnp.where(m, a * u_s + u, u)
        a = jnp.where(m, a * a_s, a)
    return a, u


def _lru_kernel(x_ref, xp_ref, xn_ref, cw_ref, cb_ref, gw_ref, gb_ref, lam_ref, h0_ref, out_ref, st_ref,
                af, uf, ab, ub, gg, *, seq, tr):
    j = pl.program_id(1)
    nt = seq // tr
    xg = x_ref[...]
    x = xg[:, 0:LRU_WIDTH]
    prev = jnp.where(j > 0, xp_ref[...], 0.0)
    nxt = jnp.where(j < nt - 1, xn_ref[...], 0.0)
    xe = jnp.concatenate([prev, x, nxt], axis=0)
    ne = tr + 16
    cw = cw_ref[...]
    xc = (cw[0:1] * pltpu.roll(xe, 2, 0)[8:8 + tr] + cw[1:2] * pltpu.roll(xe, 1, 0)[8:8 + tr]
          + cw[2:3] * x + cw[3:4] * pltpu.roll(xe, ne - 1, 0)[8:8 + tr] + cb_ref[...])
    sg = 0.5 * jnp.tanh(jnp.dot(xc.astype(BF16), gw_ref[...], preferred_element_type=F32) + gb_ref[...]) + 0.5
    z = -lam_ref[...]
    neg_c_softplus = -LRU_C * (jnp.maximum(z, 0.0) + jnp.log1p(jnp.exp(-jnp.abs(z))))
    r0 = pl.multiple_of(j * tr, tr)
    for d, (a_s, u_s) in enumerate(((af, uf), (ab, ub))):
        r = sg[:, 2 * LRU_WIDTH * d:2 * LRU_WIDTH * d + LRU_WIDTH]
        i = sg[:, 2 * LRU_WIDTH * d + LRU_WIDTH:2 * LRU_WIDTH * (d + 1)]
        log_a = neg_c_softplus[d:d + 1] * r
        a = jnp.exp(log_a)
        a_s[pl.ds(r0, tr), :] = a
        u_s[pl.ds(r0, tr), :] = jnp.sqrt(-jnp.tanh(log_a) * (a * a + 1.0)) * (i * xc)
    gg[pl.ds(r0, tr), :] = jax.nn.gelu(xg[:, LRU_WIDTH:2 * LRU_WIDTH])

    @pl.when(j == nt - 1)
    def _():
        row = lax.broadcasted_iota(jnp.int32, (8, LRU_WIDTH), 0)
        nchunk = seq // 8

        def body(c, carry):
            hf, hb = carry
            rf = pl.multiple_of(c * 8, 8)
            a, u = _chunk_scan(af[pl.ds(rf, 8), :], uf[pl.ds(rf, 8), :], row, False)
            hs = u + a * hf
            uf[pl.ds(rf, 8), :] = hs
            rb = pl.multiple_of((nchunk - 1 - c) * 8, 8)
            a2, u2 = _chunk_scan(ab[pl.ds(rb, 8), :], ub[pl.ds(rb, 8), :], row, True)
            hs2 = u2 + a2 * hb
            ub[pl.ds(rb, 8), :] = hs2
            return hs[7:8, :], hs2[0:1, :]

        h0 = h0_ref[...]
        hf, hb = lax.fori_loop(0, nchunk, body, (h0[0:1], h0[1:2]), unroll=4)
        st_ref[...] = jnp.concatenate([hf, hb], axis=0)

        def obody(t, carry):
            r = pl.multiple_of(t * tr, tr)
            out_ref[pl.ds(r, tr), :] = (gg[pl.ds(r, tr), :] * (uf[pl.ds(r, tr), :] + ub[pl.ds(r, tr), :])).astype(BF16)
            return carry

        lax.fori_loop(0, nt, obody, 0)


def _lru(lru, cw, cb, gw, gb, lam, h0, *, layer, state_layer, seq, tr):
    n = lru.shape[0]
    b = n // seq
    nt = seq // tr
    nb8 = n // 8
    per_layer = lambda bi, j: (layer, 0, 0)
    in_specs = [pl.BlockSpec((tr, LRU_IN), lambda bi, j: (bi * nt + j, 0)),
                pl.BlockSpec((8, LRU_WIDTH), lambda bi, j: (jnp.maximum((bi * nt + j) * (tr // 8) - 1, 0), 0)),
                pl.BlockSpec((8, LRU_WIDTH), lambda bi, j: (jnp.minimum((bi * nt + j + 1) * (tr // 8), nb8 - 1), 0)),
                pl.BlockSpec((None, 4, LRU_WIDTH), per_layer),
                pl.BlockSpec((None, 1, LRU_WIDTH), per_layer),
                pl.BlockSpec((None, LRU_WIDTH, 4 * LRU_WIDTH), per_layer),
                pl.BlockSpec((None, 1, 4 * LRU_WIDTH), per_layer),
                pl.BlockSpec((None, 2, LRU_WIDTH), per_layer),
                pl.BlockSpec((None, None, 2, LRU_WIDTH), lambda bi, j: (bi, state_layer, 0, 0))]
    out_specs = [pl.BlockSpec((seq, LRU_WIDTH), lambda bi, j: (bi, 0)),
                 pl.BlockSpec((None, 2, LRU_WIDTH), lambda bi, j: (bi, 0, 0))]
    out_shape = [jax.ShapeDtypeStruct((n, LRU_WIDTH), BF16),
                 jax.ShapeDtypeStruct((b, 2, LRU_WIDTH), F32)]
    return pl.pallas_call(
        functools.partial(_lru_kernel, seq=seq, tr=tr),
        grid=(b, nt), in_specs=in_specs, out_specs=out_specs, out_shape=out_shape,
        scratch_shapes=[pltpu.VMEM((seq, LRU_WIDTH), F32)] * 5,
        compiler_params=_params("arbitrary", "arbitrary"), name="lru",
    )(lru, lru, lru, cw, cb, gw, gb, lam, h0)


_NT = (((1,), (1,)), ((), ()))


def _attn_kernel(*refs, n_ctx, seq, nseq, tq, pack, ways, lam_init):
    n_in = 11 if n_ctx else 7
    if n_ctx:
        q_ref, k_ref, v_ref, dk_ref, dv_ref, ck_ref, cv_ref, cdk_ref, cdv_ref, dl_ref, dg_ref = refs[:n_in]
    else:
        q_ref, k_ref, v_ref, dk_ref, dv_ref, dl_ref, dg_ref = refs[:n_in]
    out_ref, ks, vts = refs[n_in:n_in + 3]
    s_bufs = refs[n_in + 3:n_in + 3 + ways]
    e_bufs = refs[n_in + 3 + ways:n_in + 3 + 2 * ways]
    o_buf = refs[n_in + 3 + 2 * ways]
    t_all = n_ctx + seq
    n_chunks = t_all // KEY_CHUNK

    def put(slab, new, ctx):
        if n_ctx:
            ks[slab, 0:n_ctx, :] = ctx[0].astype(BF16)
            vts[slab, :, 0:n_ctx] = ctx[1].astype(BF16).T
        ks[slab, n_ctx:t_all, :] = new[0].astype(BF16)
        vts[slab, :, n_ctx:t_all] = new[1].astype(BF16).T

    for s in range(nseq):
        rs = slice(seq * s, seq * (s + 1))
        put(3 * s, (k_ref[rs, :], v_ref[rs, :]), (ck_ref[...], cv_ref[...]) if n_ctx else None)
        for c in range(2):
            cols = slice(LANES * c, LANES * (c + 1))
            put(3 * s + 1 + c, (dk_ref[rs, cols], dv_ref[rs, cols]),
                (cdk_ref[:, cols], cdv_ref[:, cols]) if n_ctx else None)

    width = pack * tq
    upb = N_JOBS // pack
    blocks_per_seq = seq // tq
    n_units = upb * blocks_per_seq * nseq

    def slab_of(t):
        g = (t % upb) * pack
        return 3 * (t // (upb * blocks_per_seq)) + jnp.where(g < GQA_HEADS, 0, 1 + (g - GQA_HEADS) // 4)

    def q_unit(t):
        u, qb = t % upb, t // upb
        r0 = qb * tq if isinstance(t, int) else pl.multiple_of(qb * tq, tq)
        if pack == 1:
            return q_ref[u, pl.ds(r0, tq), :]
        return q_ref[pl.ds(u * pack, pack), pl.ds(r0, tq), :].reshape(width, LANES)

    def fold8(x, op):
        acc = x[0:8]
        for r in range(1, KEY_CHUNK // 8):
            acc = op(acc, x[8 * r:8 * (r + 1)])
        return acc

    def run(t, par, m_prev, l_prev, do_a=True, do_b=True, do_c=True):
        ga, gb, gc = t, t - ways, t - 2 * ways
        s_buf, e_buf = s_bufs[par], e_bufs[par]
        m_acc = l_acc = None
        if do_a:
            qa = q_unit(ga)
            ka = slab_of(ga)
            m_acc = jnp.full((8, width), -jnp.inf, F32)
        if do_b:
            m_row = jnp.max(m_prev, axis=0, keepdims=True)
            l_acc = jnp.zeros((8, width), F32)
        if do_c:
            vc = slab_of(gc)
            o_acc = jnp.zeros((LANES, width), F32)
        for c in range(n_chunks):
            rows = slice(KEY_CHUNK * c, KEY_CHUNK * (c + 1))
            if do_c:
                o_acc = o_acc + jnp.dot(vts[vc, :, rows], e_buf[rows, :], preferred_element_type=F32)
            if do_b:
                e = jnp.exp2(s_buf[rows, :] - m_row)
                e_buf[rows, :] = e.astype(BF16)
                l_acc = l_acc + fold8(e, jnp.add)
            if do_a:
                s = lax.dot_general(ks[ka, rows, :], qa, _NT, preferred_element_type=F32)
                s_buf[rows, :] = s
                m_acc = jnp.maximum(m_acc, fold8(s, jnp.maximum))
        if do_c:
            o_buf[gc % upb] = o_acc * (1.0 / jnp.sum(l_prev, axis=0, keepdims=True))
        return m_acc, l_acc

    def job_out(g, rows):
        return o_buf[g // pack, rows, tq * (g % pack):tq * (g % pack + 1)]

    def finish_block(qb):
        r0 = qb * tq if isinstance(qb, int) else pl.multiple_of(qb * tq, tq)
        out_rows = pl.ds(r0, tq)
        for c in range(4):
            h = c // 2
            rows = slice(HEAD_DIM * h, HEAD_DIM * (h + 1))
            ot = jnp.concatenate([job_out(2 * c, rows), job_out(2 * c + 1, rows)], axis=0)
            out_ref[out_rows, LANES * c:LANES * (c + 1)] = ot.astype(BF16).T

        dl = dl_ref[...]
        lam = (jnp.exp(jnp.sum(dl[0:1] * dl[1:2], axis=-1, keepdims=True))
               - jnp.exp(jnp.sum(dl[2:3] * dl[3:4], axis=-1, keepdims=True)) + lam_init)

        gain = dg_ref[...] * (1.0 - lam_init)
        for c in range(2):
            parts = []
            for par in range(2):
                g1 = GQA_HEADS + 2 * (2 * c + par)
                rows = slice(HEAD_DIM * par, HEAD_DIM * (par + 1))
                o = job_out(g1, rows) - lam * job_out(g1 + 1, rows)
                parts.append(o * lax.rsqrt(jnp.mean(o * o, axis=0, keepdims=True) + EPS))
            ot = jnp.concatenate(parts, axis=0)
            ot = jnp.concatenate([ot[:, LANES * j:LANES * (j + 1)] * gain for j in range(tq // LANES)], axis=1)
            out_ref[out_rows, GQA_WIDTH + LANES * c:GQA_WIDTH + LANES * (c + 1)] = ot.astype(BF16).T

    bpb = upb // ways
    n_bodies = n_units // ways
    nothing = (None,) * ways

    def body(i, carry, **stages):
        ms, ls = [], []
        for par in range(ways):
            m, l = run(ways * i + par, par, carry[0][par], carry[1][par], **stages)
            ms.append(m)
            ls.append(l)
        if isinstance(i, int):
            if i > 1 and (i - 1) % bpb == 0:
                finish_block((i - 1) // bpb - 1)
        elif n_bodies > bpb:
            @pl.when((i - 1) % bpb == 0)
            def _():
                finish_block((i - 1) // bpb - 1)
        return tuple(ms), tuple(ls)

    carry = body(0, (nothing, nothing), do_b=False, do_c=False)
    carry = body(1, (carry[0], nothing), do_c=False)
    carry = lax.fori_loop(2, n_bodies, body, carry)
    carry = body(n_bodies, carry, do_a=False)
    body(n_bodies + 1, (nothing, carry[1]), do_a=False, do_b=False)


def _attention(qx, k, v, dk, dv, caches, dl, dg, *, layer, seq, nseq, tq, lam_init):
    pack = ATTN_WIDTH // tq
    ways = 2 if seq > ATTN_WIDTH else 4
    n = k.shape[0]
    rows = nseq * seq
    n_ctx = 0 if caches is None else caches[0].shape[2]
    assert n_ctx == 0 or nseq == 1
    t_all = n_ctx + seq
    per_b = lambda bi: (bi, 0)
    in_specs = [pl.BlockSpec((N_JOBS, rows, LANES), lambda bi: (0, bi, 0)),
                pl.BlockSpec((rows, KV_WIDTH), per_b),
                pl.BlockSpec((rows, KV_WIDTH), per_b),
                pl.BlockSpec((rows, DIFF_WIDTH), per_b),
                pl.BlockSpec((rows, DIFF_WIDTH), per_b)]
    args = [qx, k, v, dk, dv]
    if n_ctx:
        cache_idx = lambda bi: (bi, layer, 0, 0)
        in_specs += [pl.BlockSpec((None, None, n_ctx, KV_WIDTH), cache_idx),
                     pl.BlockSpec((None, None, n_ctx, KV_WIDTH), cache_idx),
                     pl.BlockSpec((None, None, n_ctx, DIFF_WIDTH), cache_idx),
                     pl.BlockSpec((None, None, n_ctx, DIFF_WIDTH), cache_idx)]
        args += list(caches)
    in_specs += [pl.BlockSpec((None, 4, DIFF_QK_DIM), lambda bi: (layer, 0, 0)),
                 pl.BlockSpec((None, LANES, LANES), lambda bi: (layer, 0, 0))]
    args += [dl, dg]
    return pl.pallas_call(
        functools.partial(_attn_kernel, n_ctx=n_ctx, seq=seq, nseq=nseq, tq=tq, pack=pack, ways=ways,
                          lam_init=lam_init),
        grid=(n // rows,), in_specs=in_specs,
        out_specs=pl.BlockSpec((rows, D_MODEL - LRU_WIDTH), per_b),
        out_shape=jax.ShapeDtypeStruct((n, D_MODEL - LRU_WIDTH), BF16),
        scratch_shapes=[pltpu.VMEM((3 * nseq, t_all, LANES), BF16), pltpu.VMEM((3 * nseq, LANES, t_all), BF16)]
        + [pltpu.VMEM((t_all, ATTN_WIDTH), F32)] * ways + [pltpu.VMEM((t_all, ATTN_WIDTH), BF16)] * ways
        + [pltpu.VMEM((N_JOBS // pack, LANES, ATTN_WIDTH), F32)],
        compiler_params=_params("arbitrary"), name="attention",
    )(*args)


HALO = 16


def _zero_row(x, r):
    g = r // 8 * 8
    row = lax.broadcasted_iota(jnp.int32, (8, x.shape[1]), 0)
    patched = jnp.where(row == r - g, 0.0, x[g:g + 8])
    return jnp.concatenate([x[:g], patched, x[g + 8:]], axis=0)


def _mix_ffn_kernel(lru_ref, lrup_ref, lrun_ref, att_ref, attp_ref, attn_ref, x_ref, xp_ref, xn_ref,
                    wo_ref, n2_ref, up_ref, cw_ref, cb_ref, wd_ref, mod_ref, fin_ref, o_ref,
                    mix, xe, he, act, *, seq, tm, final):
    i = pl.program_id(0)
    nts = max(seq // tm, 1)
    pos = i % nts
    ne = tm + 2 * HALO
    for r0, nr, lru_r, att_r, x_r in ((0, HALO, lrup_ref, attp_ref, xp_ref),
                                      (HALO, tm, lru_ref, att_ref, x_ref),
                                      (HALO + tm, HALO, lrun_ref, attn_ref, xn_ref)):
        mix[r0:r0 + nr, 0:LRU_WIDTH] = lru_r[...]
        mix[r0:r0 + nr, LRU_WIDTH:D_MODEL] = att_r[...]
        xe[r0:r0 + nr, :] = x_r[...]
    mod = mod_ref[...]
    x1 = xe[...] + mod[:, 2 * D_MODEL:3 * D_MODEL] * jnp.dot(mix[...], wo_ref[...], preferred_element_type=F32)
    xe[...] = x1
    he[...] = (_rms(x1) * (n2_ref[...] * (1.0 + mod[:, 4 * D_MODEL:5 * D_MODEL]))
               + mod[:, 3 * D_MODEL:4 * D_MODEL]).astype(BF16)

    @pl.when(pos == 0)
    def _():
        he[0:HALO, :] = jnp.zeros((HALO, D_MODEL), BF16)

    @pl.when(pos == nts - 1)
    def _():
        he[HALO + tm:ne, :] = jnp.zeros((HALO, D_MODEL), BF16)

    def up(c, half, scale):
        cols = slice(half * D_FF + FFN_CHUNK * c, half * D_FF + FFN_CHUNK * (c + 1))
        u = jnp.dot(he[...], up_ref[:, cols], preferred_element_type=F32)
        cw, cb = cw_ref[:, cols], cb_ref[:, cols]
        if scale != 1.0:
            cw, cb = cw * scale, cb * scale
        before = pltpu.roll(u, 1, 0)[HALO:HALO + tm]
        after = pltpu.roll(u, ne - 1, 0)[HALO:HALO + tm]
        for b in range(seq, tm, seq):
            before = _zero_row(before, b)
            after = _zero_row(after, b - 1)
        return cw[0:1] * before + cw[1:2] * u[HALO:HALO + tm] + cw[2:3] * after + cb

    for c in range(N_FFN_CHUNKS):
        a, hg = up(c, 0, 1.0), up(c, 1, 0.5)
        act[:, FFN_CHUNK * c:FFN_CHUNK * (c + 1)] = ((hg * jnp.tanh(hg) + hg) * a).astype(BF16)
    down = jnp.dot(act[...], wd_ref[...], preferred_element_type=F32)
    x2 = xe[HALO:HALO + tm, :] + mod[:, 5 * D_MODEL:6 * D_MODEL] * down
    if final:
        x2 = _rms(x2) * fin_ref[...]
    o_ref[...] = x2


def _mix_ffn(lru_o, att_o, x, w_out, n2, up, cw, cb, wd, mod, fin_g, *, layer, seq, tm, mod_row, final):
    n = x.shape[0]
    assert seq % tm == 0 or tm % seq == 0
    nts = max(seq // tm, 1)
    nbh = n // HALO
    row = lambda i: (i, 0)
    prev = lambda i: (jnp.maximum(i * (tm // HALO) - 1, 0), 0)
    nxt = lambda i: (jnp.minimum((i + 1) * (tm // HALO), nbh - 1), 0)
    c2 = lambda i: (0, 0)
    per_layer = lambda i: (layer, 0, 0)
    att_w = D_MODEL - LRU_WIDTH

    def with_halo(width):
        return [pl.BlockSpec((tm, width), row), pl.BlockSpec((HALO, width), prev), pl.BlockSpec((HALO, width), nxt)]

    in_specs = (with_halo(LRU_WIDTH) + with_halo(att_w) + with_halo(D_MODEL)
                + [pl.BlockSpec((None, D_MODEL, D_MODEL), per_layer),
                   pl.BlockSpec((None, 1, D_MODEL), per_layer),
                   pl.BlockSpec((None, D_MODEL, 2 * D_FF), per_layer),
                   pl.BlockSpec((None, 3, 2 * D_FF), per_layer),
                   pl.BlockSpec((None, 1, 2 * D_FF), per_layer),
                   pl.BlockSpec((None, D_FF, D_MODEL), per_layer),
                   pl.BlockSpec((None, 1, 6 * D_MODEL), lambda i: (layer * COND_ROWS + mod_row(i // nts), 0, 0)),
                   pl.BlockSpec((1, D_MODEL), c2)])
    return pl.pallas_call(
        functools.partial(_mix_ffn_kernel, seq=seq, tm=tm, final=final),
        grid=(n // tm,), in_specs=in_specs,
        out_specs=pl.BlockSpec((tm, D_MODEL), row),
        out_shape=jax.ShapeDtypeStruct((n, D_MODEL), F32),
        scratch_shapes=[pltpu.VMEM((tm + 2 * HALO, D_MODEL), BF16), pltpu.VMEM((tm + 2 * HALO, D_MODEL), F32),
                        pltpu.VMEM((tm + 2 * HALO, D_MODEL), BF16), pltpu.VMEM((tm, D_FF), BF16)],
        compiler_params=_params("arbitrary"), name="mix_ffn",
    )(lru_o, lru_o, lru_o, att_o, att_o, att_o, x, x, x, w_out, n2, up, cw, cb, wd, mod, fin_g)


def _rope_tables(seq, dim):
    rows = seq // GRID_W
    t_row = np.repeat(np.arange(rows, dtype=np.float64), GRID_W)
    t_col = np.tile(np.arange(GRID_W, dtype=np.float64), rows)
    axis_dim = dim // 2
    inv = ROPE_THETA ** (-np.arange(0, axis_dim, 2, dtype=np.float64) / axis_dim)
    ar = t_row[:, None] * inv
    ac = t_col[:, None] * inv
    ang = np.concatenate([ar, ar, ac, ac], axis=-1)
    reps = LANES // dim
    cos = np.tile(np.cos(ang), (1, reps))
    sin = np.tile(np.sin(ang), (1, reps))
    first = (np.arange(LANES) % (dim // 2)) < (dim // 4)
    sin_neg = np.where(first, -sin, 0.0)
    sin_pos = np.where(first, 0.0, sin)
    return tuple(jnp.asarray(t, F32) for t in (cos, sin_neg, sin_pos))


def _gate_weights(w, b):
    nl = w.shape[0]
    eye = jnp.eye(LRU_BLOCKS, dtype=w.dtype)
    wp = w.reshape(nl, 2, LRU_BLOCKS, LRU_BLOCK_W, 2, LRU_BLOCK_W)
    dense = jnp.einsum('lpndqe,nm->lndpqme', wp, eye).reshape(nl, LRU_WIDTH, 4 * LRU_WIDTH)
    bias = jnp.transpose(b.reshape(nl, 2, LRU_BLOCKS, 2, LRU_BLOCK_W), (0, 1, 3, 2, 4)).reshape(nl, 1, 4 * LRU_WIDTH)
    return (0.5 * dense).astype(BF16), 0.5 * bias


def kernel(x_prompt, x_sample, cache_gqa_k, cache_gqa_v, cache_diff_k, cache_diff_v, state_lru, c, c_ctx,
           norm1_g, norm2_g, final_norm_g, ada_w, ada_b, w_in, w_out, lru_conv_w, lru_conv_b, lru_gate_w,
           lru_gate_b, lru_lambda, gqa_q_norm_g, gqa_k_norm_g, diff_lambda, diff_norm_g, ffn_w_up, ffn_conv_w,
           ffn_conv_b, ffn_w_down):
    bp, sp, _ = x_prompt.shape
    bs, ss, _ = x_sample.shape
    n_ctx = cache_gqa_k.shape[2]

    cond = jnp.concatenate([c_ctx[None, :], c, jnp.zeros((COND_ROWS - 1 - bs, D_MODEL), F32)], axis=0)
    mod_all = _ada(cond, ada_w, ada_b)

    caches = (cache_gqa_k.reshape(bs, DEPTH, n_ctx, KV_WIDTH), cache_gqa_v.reshape(bs, DEPTH, n_ctx, KV_WIDTH),
              cache_diff_k.reshape(bs, DEPTH, n_ctx, DIFF_WIDTH), cache_diff_v.reshape(bs, DEPTH, n_ctx, DIFF_WIDTH))
    tables = _rope_tables(ss, HEAD_DIM) + _rope_tables(ss, DIFF_QK_DIM)
    ones_blk = jnp.asarray(np.kron(np.eye(2), np.ones((HEAD_DIM, HEAD_DIM))), BF16)
    zero_state = jnp.zeros((bp, 1, 2, LRU_WIDTH), F32)

    groups = {
        'p': dict(seq=sp, tm=512, tq=256, nseq=4, mod_row=lambda b: 0, tables=None, caches=None),
        's': dict(seq=ss, tm=512, tq=512, nseq=1, mod_row=lambda b: 1 + b, tables=tables, caches=caches),
    }
    xs = {'p': x_prompt.reshape(bp * sp, D_MODEL), 's': x_sample.reshape(bs * ss, D_MODEL)}
    new_k, new_v, new_dk, new_dv, new_st = [], [], [], [], []

    mod = mod_all.reshape(DEPTH * COND_ROWS, 1, 6 * D_MODEL)
    w_in_b = w_in.astype(BF16)
    w_out_b = w_out.astype(BF16)
    up_b = ffn_w_up.astype(BF16)
    wd_b = ffn_w_down.astype(BF16)
    ffn_cb = ffn_conv_b.reshape(DEPTH, 1, 2 * D_FF)
    fin = final_norm_g[None, :]
    gw, gb = _gate_weights(lru_gate_w, lru_gate_b)
    lru_cb = lru_conv_b.reshape(DEPTH, 1, LRU_WIDTH)
    qg = jnp.tile(gqa_q_norm_g, (1, 2)).reshape(DEPTH, 1, LANES)
    kg = jnp.tile(gqa_k_norm_g, (1, 2)).reshape(DEPTH, 1, LANES)
    dg = jnp.broadcast_to(jnp.tile(diff_norm_g, (1, 2))[:, :, None], (DEPTH, LANES, LANES))
    n1 = norm1_g.reshape(DEPTH, 1, D_MODEL)
    n2 = norm2_g.reshape(DEPTH, 1, D_MODEL)

    for l in range(DEPTH):
        lam_init = 0.8 - 0.6 * math.exp(-0.3 * l)
        for name in ('p', 's'):
            g = groups[name]
            seq, tm, mod_row = g['seq'], g['tm'], g['mod_row']
            x = xs[name]
            lru, qx, k, v, dk, dv = _norm_proj(x, mod, n1, w_in_b, qg, kg, ones_blk, g['tables'],
                                               layer=l, seq=seq, tm=1024 if seq % 1024 == 0 else 512,
                                               mod_row=mod_row, kv_dtype=F32 if name == 'p' else BF16)
            h0, state_layer = (zero_state, 0) if name == 'p' else (state_lru, l)
            lru_o, st = _lru(lru, lru_conv_w, lru_cb, gw, gb, lru_lambda, h0,
                             layer=l, state_layer=state_layer, seq=seq, tr=min(seq, 512))
            att_o = _attention(qx, k, v, dk, dv, g['caches'], diff_lambda, dg,
                               layer=l, seq=seq, nseq=g['nseq'], tq=g['tq'], lam_init=lam_init)
            xs[name] = _mix_ffn(lru_o, att_o, x, w_out_b, n2, up_b, ffn_conv_w, ffn_cb, wd_b, mod, fin,
                                layer=l, seq=seq, tm=tm, mod_row=mod_row, final=(l == DEPTH - 1))
            if name == 'p':
                new_k.append(k)
                new_v.append(v)
                new_dk.append(dk)
                new_dv.append(dv)
                new_st.append(st)

    y_prompt = xs['p'].reshape(bp, sp, D_MODEL)
    y_sample = xs['s'].reshape(bs, ss, D_MODEL)
    stack = lambda parts, shape: jnp.stack([p.reshape((bp,) + shape) for p in parts], axis=1)
    return (y_prompt, y_sample,
            stack(new_k, (sp, GQA_KV_HEADS, HEAD_DIM)),
            stack(new_v, (sp, GQA_KV_HEADS, HEAD_DIM)),
            stack(new_dk, (sp, DIFF_HEADS, 2, DIFF_QK_DIM)),
            stack(new_dv, (sp, DIFF_HEADS, HEAD_DIM)),
            jnp.stack(new_st, axis=1))
```

```python
import functools
import math

import numpy as np
import jax
import jax.numpy as jnp
from jax import lax
from jax.experimental import pallas as pl
from jax.experimental.pallas import tpu as pltpu

F32 = jnp.float32
BF16 = jnp.bfloat16

D_MODEL = 1024
DEPTH = 2
GRID_W = 64
HEAD_DIM = 64
LRU_WIDTH = 256
LRU_BLOCKS = 4
LRU_BLOCK_W = LRU_WIDTH // LRU_BLOCKS
LRU_C = 8.0
GQA_HEADS = 8
GQA_KV_HEADS = 2
DIFF_HEADS = 4
DIFF_QK_DIM = 32
D_FF = 2816
ROPE_THETA = 10000.0
EPS = 1e-6

LRU_IN = 2 * LRU_WIDTH
GQA_WIDTH = GQA_HEADS * HEAD_DIM
KV_WIDTH = GQA_KV_HEADS * HEAD_DIM
DIFF_WIDTH = DIFF_HEADS * HEAD_DIM
Q_OFF = LRU_IN
KV_OFF = Q_OFF + GQA_WIDTH
IN_WIDTH = KV_OFF + 2 * KV_WIDTH + 3 * DIFF_WIDTH
LOG2_E = math.log2(math.e)
N_JOBS = GQA_HEADS + 2 * DIFF_HEADS
KEY_CHUNK = 256
ATTN_WIDTH = 512
LANES = 128
FFN_CHUNK = 256
N_FFN_CHUNKS = D_FF // FFN_CHUNK
COND_ROWS = 8
VMEM_LIMIT = 56 * 2 ** 20


def _params(*sem):
    return pltpu.CompilerParams(dimension_semantics=sem, vmem_limit_bytes=VMEM_LIMIT)


def _rms(x):
    return x * lax.rsqrt(jnp.mean(x * x, axis=-1, keepdims=True) + EPS)


def _ada_kernel(cond_ref, w_ref, b_ref, o_ref):
    c = cond_ref[...]
    s = c * jax.nn.sigmoid(c)
    o_ref[...] = jnp.dot(s.astype(BF16), w_ref[...].astype(BF16), preferred_element_type=F32) + b_ref[...]


def _ada(cond, ada_w, ada_b):
    tn = 1536
    width = 6 * D_MODEL
    return pl.pallas_call(
        _ada_kernel,
        grid=(DEPTH, width // tn),
        in_specs=[pl.BlockSpec((COND_ROWS, D_MODEL), lambda l, j: (0, 0)),
                  pl.BlockSpec((None, D_MODEL, tn), lambda l, j: (l, 0, j)),
                  pl.BlockSpec((None, 1, tn), lambda l, j: (l, 0, j))],
        out_specs=pl.BlockSpec((None, COND_ROWS, tn), lambda l, j: (l, 0, j)),
        out_shape=jax.ShapeDtypeStruct((DEPTH, COND_ROWS, width), F32),
        compiler_params=_params("arbitrary", "arbitrary"),
        name="ada",
    )(cond, ada_w, ada_b.reshape(DEPTH, 1, width))


def _group_mean_sq(x, ones_blk):
    sq = x * x
    hi = sq.astype(BF16)
    lo = (sq - hi.astype(F32)).astype(BF16)
    s = jnp.dot(hi, ones_blk, preferred_element_type=F32) + jnp.dot(lo, ones_blk, preferred_element_type=F32)
    return s * (1.0 / HEAD_DIM)


def _rope(x, cos, sin_neg, sin_pos, quarter):
    return x * cos + pltpu.roll(x, LANES - quarter, 1) * sin_neg + pltpu.roll(x, quarter, 1) * sin_pos


def _norm_proj_kernel(*refs, rope):
    if rope:
        (x_ref, mod_ref, n1_ref, w_ref, qg_ref, kg_ref, ones_ref, cq, snq, spq, cd, snd, spd,
         lru_ref, qx_ref, k_ref, v_ref, dk_ref, dv_ref) = refs
    else:
        (x_ref, mod_ref, n1_ref, w_ref, qg_ref, kg_ref, ones_ref,
         lru_ref, qx_ref, k_ref, v_ref, dk_ref, dv_ref) = refs
    mod = mod_ref[...]
    h = _rms(x_ref[...]) * (n1_ref[...] * (1.0 + mod[:, D_MODEL:2 * D_MODEL])) + mod[:, 0:D_MODEL]
    hb = h.astype(BF16)
    ones_blk = ones_ref[...]
    lane = lax.broadcasted_iota(jnp.int32, (1, LANES), 1)

    proj = jnp.dot(hb, w_ref[:, Q_OFF:KV_OFF], preferred_element_type=F32)
    for c in range(4):
        xc = proj[:, LANES * c:LANES * (c + 1)]
        xc = xc * lax.rsqrt(_group_mean_sq(xc, ones_blk) + EPS) * qg_ref[...]
        if rope:
            xc = _rope(xc, cq[...], snq[...], spq[...], HEAD_DIM // 4)
        xc = xc * (HEAD_DIM ** -0.5 * LOG2_E)
        xr = pltpu.roll(xc, HEAD_DIM, 1)
        for par in range(2):
            j = 2 * c + par
            want = j // (GQA_HEADS // GQA_KV_HEADS)
            src = xc if par == want else xr
            qx_ref[j] = jnp.where(lane // HEAD_DIM == want, src, 0.0).astype(BF16)

    proj = jnp.dot(hb, w_ref[:, KV_OFF:IN_WIDTH], preferred_element_type=F32)
    dq_off = 2 * KV_WIDTH
    dk_off = dq_off + DIFF_WIDTH
    dv_off = dk_off + DIFF_WIDTH
    kc = proj[:, 0:KV_WIDTH]
    kc = kc * lax.rsqrt(_group_mean_sq(kc, ones_blk) + EPS) * kg_ref[...]
    if rope:
        kc = _rope(kc, cq[...], snq[...], spq[...], HEAD_DIM // 4)
    k_ref[...] = kc.astype(k_ref.dtype)
    v_ref[...] = proj[:, KV_WIDTH:2 * KV_WIDTH].astype(v_ref.dtype)

    for c in range(2):
        xc = proj[:, dq_off + LANES * c:dq_off + LANES * (c + 1)]
        if rope:
            xc = _rope(xc, cd[...], snd[...], spd[...], DIFF_QK_DIM // 4)
        xc = xc * (DIFF_QK_DIM ** -0.5 * LOG2_E)
        for g in range(4):
            qx_ref[GQA_HEADS + 4 * c + g] = jnp.where(lane // DIFF_QK_DIM == g, xc, 0.0).astype(BF16)
        kc = proj[:, dk_off + LANES * c:dk_off + LANES * (c + 1)]
        if rope:
            kc = _rope(kc, cd[...], snd[...], spd[...], DIFF_QK_DIM // 4)
        dk_ref[:, LANES * c:LANES * (c + 1)] = kc.astype(dk_ref.dtype)
    dv_ref[...] = proj[:, dv_off:dv_off + DIFF_WIDTH].astype(dv_ref.dtype)

    lru_ref[...] = jnp.dot(hb, w_ref[:, 0:LRU_IN], preferred_element_type=F32)


def _norm_proj(x, mod, n1, w_in, qg, kg, ones_blk, tables, *, layer, seq, tm, mod_row, kv_dtype):
    n = x.shape[0]
    rope = tables is not None
    nts = seq // tm
    row = lambda i: (i, 0)
    const = lambda i: (0, 0)
    per_layer = lambda i: (layer, 0, 0)
    in_specs = [pl.BlockSpec((tm, D_MODEL), row),
                pl.BlockSpec((None, 1, 6 * D_MODEL), lambda i: (layer * COND_ROWS + mod_row(i * tm // seq), 0, 0)),
                pl.BlockSpec((None, 1, D_MODEL), per_layer),
                pl.BlockSpec((None, D_MODEL, IN_WIDTH), per_layer),
                pl.BlockSpec((None, 1, LANES), per_layer),
                pl.BlockSpec((None, 1, LANES), per_layer),
                pl.BlockSpec((LANES, LANES), const)]
    args = [x, mod, n1, w_in, qg, kg, ones_blk]
    if rope:
        in_specs += [pl.BlockSpec((tm, LANES), lambda i: (i % nts, 0))] * 6
        args += list(tables)
    slab = lambda i: (0, i, 0)
    widths = (LRU_IN, KV_WIDTH, KV_WIDTH, DIFF_WIDTH, DIFF_WIDTH)
    out_specs = [pl.BlockSpec((tm, wd), row) for wd in widths]
    out_shape = [jax.ShapeDtypeStruct((n, wd), F32 if j == 0 else kv_dtype) for j, wd in enumerate(widths)]
    out_specs.insert(1, pl.BlockSpec((N_JOBS, tm, LANES), slab))
    out_shape.insert(1, jax.ShapeDtypeStruct((N_JOBS, n, LANES), BF16))
    return pl.pallas_call(
        functools.partial(_norm_proj_kernel, rope=rope),
        grid=(n // tm,), in_specs=in_specs, out_specs=out_specs, out_shape=out_shape,
        compiler_params=_params("arbitrary"), name="norm_proj",
    )(*args)


def _chunk_scan(a, u, row, reverse):
    for d in (1, 2, 4):
        shift = 8 - d if reverse else d
        a_s = pltpu.roll(a, shift, 0)
        u_s = pltpu.roll(u, shift, 0)
        m = (row < 8 - d) if reverse else (row >= d)
        u = jnp.where(m, a * u_s + u, u)
        a = jnp.where(m, a * a_s, a)
    return a, u


def _lru_kernel(x_ref, xp_ref, xn_ref, cw_ref, cb_ref, gw_ref, gb_ref, lam_ref, h0_ref, out_ref, st_ref,
                af, uf, ab, ub, gg, *, seq, tr):
    j = pl.program_id(1)
    nt = seq // tr
    xg = x_ref[...]
    x = xg[:, 0:LRU_WIDTH]
    prev = jnp.where(j > 0, xp_ref[...], 0.0)
    nxt = jnp.where(j < nt - 1, xn_ref[...], 0.0)
    xe = jnp.concatenate([prev, x, nxt], axis=0)
    ne = tr + 16
    cw = cw_ref[...]
    xc = (cw[0:1] * pltpu.roll(xe, 2, 0)[8:8 + tr] + cw[1:2] * pltpu.roll(xe, 1, 0)[8:8 + tr]
          + cw[2:3] * x + cw[3:4] * pltpu.roll(xe, ne - 1, 0)[8:8 + tr] + cb_ref[...])
    sg = 0.5 * jnp.tanh(jnp.dot(xc.astype(BF16), gw_ref[...], preferred_element_type=F32) + gb_ref[...]) + 0.5
    z = -lam_ref[...]
    neg_c_softplus = -LRU_C * (jnp.maximum(z, 0.0) + jnp.log1p(jnp.exp(-jnp.abs(z))))
    r0 = pl.multiple_of(j * tr, tr)
    for d, (a_s, u_s) in enumerate(((af, uf), (ab, ub))):
        r = sg[:, 2 * LRU_WIDTH * d:2 * LRU_WIDTH * d + LRU_WIDTH]
        i = sg[:, 2 * LRU_WIDTH * d + LRU_WIDTH:2 * LRU_WIDTH * (d + 1)]
        log_a = neg_c_softplus[d:d + 1] * r
        a = jnp.exp(log_a)
        a_s[pl.ds(r0, tr), :] = a
        u_s[pl.ds(r0, tr), :] = jnp.sqrt(-jnp.tanh(log_a) * (a * a + 1.0)) * (i * xc)
    gg[pl.ds(r0, tr), :] = jax.nn.gelu(xg[:, LRU_WIDTH:2 * LRU_WIDTH])

    @pl.when(j == nt - 1)
    def _():
        row = lax.broadcasted_iota(jnp.int32, (8, LRU_WIDTH), 0)
        nchunk = seq // 8

        def body(c, carry):
            hf, hb = carry
            rf = pl.multiple_of(c * 8, 8)
            a, u = _chunk_scan(af[pl.ds(rf, 8), :], uf[pl.ds(rf, 8), :], row, False)
            hs = u + a * hf
            uf[pl.ds(rf, 8), :] = hs
            rb = pl.multiple_of((nchunk - 1 - c) * 8, 8)
            a2, u2 = _chunk_scan(ab[pl.ds(rb, 8), :], ub[pl.ds(rb, 8), :], row, True)
            hs2 = u2 + a2 * hb
            ub[pl.ds(rb, 8), :] = hs2
            return hs[7:8, :], hs2[0:1, :]

        h0 = h0_ref[...]
        hf, hb = lax.fori_loop(0, nchunk, body, (h0[0:1], h0[1:2]), unroll=4)
        st_ref[...] = jnp.concatenate([hf, hb], axis=0)

        def obody(t, carry):
            r = pl.multiple_of(t * tr, tr)
            out_ref[pl.ds(r, tr), :] = (gg[pl.ds(r, tr), :] * (uf[pl.ds(r, tr), :] + ub[pl.ds(r, tr), :])).astype(BF16)
            return carry

        lax.fori_loop(0, nt, obody, 0)


def _lru(lru, cw, cb, gw, gb, lam, h0, *, layer, state_layer, seq, tr):
    n = lru.shape[0]
    b = n // seq
    nt = seq // tr
    nb8 = n // 8
    per_layer = lambda bi, j: (layer, 0, 0)
    in_specs = [pl.BlockSpec((tr, LRU_IN), lambda bi, j: (bi * nt + j, 0)),
                pl.BlockSpec((8, LRU_WIDTH), lambda bi, j: (jnp.maximum((bi * nt + j) * (tr // 8) - 1, 0), 0)),
                pl.BlockSpec((8, LRU_WIDTH), lambda bi, j: (jnp.minimum((bi * nt + j + 1) * (tr // 8), nb8 - 1), 0)),
                pl.BlockSpec((None, 4, LRU_WIDTH), per_layer),
                pl.BlockSpec((None, 1, LRU_WIDTH), per_layer),
                pl.BlockSpec((None, LRU_WIDTH, 4 * LRU_WIDTH), per_layer),
                pl.BlockSpec((None, 1, 4 * LRU_WIDTH), per_layer),
                pl.BlockSpec((None, 2, LRU_WIDTH), per_layer),
                pl.BlockSpec((None, None, 2, LRU_WIDTH), lambda bi, j: (bi, state_layer, 0, 0))]
    out_specs = [pl.BlockSpec((seq, LRU_WIDTH), lambda bi, j: (bi, 0)),
                 pl.BlockSpec((None, 2, LRU_WIDTH), lambda bi, j: (bi, 0, 0))]
    out_shape = [jax.ShapeDtypeStruct((n, LRU_WIDTH), BF16),
                 jax.ShapeDtypeStruct((b, 2, LRU_WIDTH), F32)]
    return pl.pallas_call(
        functools.partial(_lru_kernel, seq=seq, tr=tr),
        grid=(b, nt), in_specs=in_specs, out_specs=out_specs, out_shape=out_shape,
        scratch_shapes=[pltpu.VMEM((seq, LRU_WIDTH), F32)] * 5,
        compiler_params=_params("arbitrary", "arbitrary"), name="lru",
    )(lru, lru, lru, cw, cb, gw, gb, lam, h0)


_NT = (((1,), (1,)), ((), ()))


def _attn_kernel(*refs, n_ctx, seq, nseq, tq, pack, ways, lam_init):
    n_in = 11 if n_ctx else 7
    if n_ctx:
        q_ref, k_ref, v_ref, dk_ref, dv_ref, ck_ref, cv_ref, cdk_ref, cdv_ref, dl_ref, dg_ref = refs[:n_in]
    else:
        q_ref, k_ref, v_ref, dk_ref, dv_ref, dl_ref, dg_ref = refs[:n_in]
    out_ref, ks, vts = refs[n_in:n_in + 3]
    s_bufs = refs[n_in + 3:n_in + 3 + ways]
    e_bufs = refs[n_in + 3 + ways:n_in + 3 + 2 * ways]
    o_buf = refs[n_in + 3 + 2 * ways]
    t_all = n_ctx + seq
    n_chunks = t_all // KEY_CHUNK

    def put(slab, new, ctx):
        if n_ctx:
            ks[slab, 0:n_ctx, :] = ctx[0].astype(BF16)
            vts[slab, :, 0:n_ctx] = ctx[1].astype(BF16).T
        ks[slab, n_ctx:t_all, :] = new[0].astype(BF16)
        vts[slab, :, n_ctx:t_all] = new[1].astype(BF16).T

    for s in range(nseq):
        rs = slice(seq * s, seq * (s + 1))
        put(3 * s, (k_ref[rs, :], v_ref[rs, :]), (ck_ref[...], cv_ref[...]) if n_ctx else None)
        for c in range(2):
            cols = slice(LANES * c, LANES * (c + 1))
            put(3 * s + 1 + c, (dk_ref[rs, cols], dv_ref[rs, cols]),
                (cdk_ref[:, cols], cdv_ref[:, cols]) if n_ctx else None)

    width = pack * tq
    upb = N_JOBS // pack
    blocks_per_seq = seq // tq
    n_units = upb * blocks_per_seq * nseq

    def slab_of(t):
        g = (t % upb) * pack
        return 3 * (t // (upb * blocks_per_seq)) + jnp.where(g < GQA_HEADS, 0, 1 + (g - GQA_HEADS) // 4)

    def q_unit(t):
        u, qb = t % upb, t // upb
        r0 = qb * tq if isinstance(t, int) else pl.multiple_of(qb * tq, tq)
        if pack == 1:
            return q_ref[u, pl.ds(r0, tq), :]
        return q_ref[pl.ds(u * pack, pack), pl.ds(r0, tq), :].reshape(width, LANES)

    def fold8(x, op):
        acc = x[0:8]
        for r in range(1, KEY_CHUNK // 8):
            acc = op(acc, x[8 * r:8 * (r + 1)])
        return acc

    def run(t, par, m_prev, l_prev, do_a=True, do_b=True, do_c=True):
        ga, gb, gc = t, t - ways, t - 2 * ways
        s_buf, e_buf = s_bufs[par], e_bufs[par]
        m_acc = l_acc = None
        if do_a:
            qa = q_unit(ga)
            ka = slab_of(ga)
            m_acc = jnp.full((8, width), -jnp.inf, F32)
        if do_b:
            m_row = jnp.max(m_prev, axis=0, keepdims=True)
            l_acc = jnp.zeros((8, width), F32)
        if do_c:
            vc = slab_of(gc)
            o_acc = jnp.zeros((LANES, width), F32)
        for c in range(n_chunks):
            rows = slice(KEY_CHUNK * c, KEY_CHUNK * (c + 1))
            if do_c:
                o_acc = o_acc + jnp.dot(vts[vc, :, rows], e_buf[rows, :], preferred_element_type=F32)
            if do_b:
                e = jnp.exp2(s_buf[rows, :] - m_row)
                e_buf[rows, :] = e.astype(BF16)
                l_acc = l_acc + fold8(e, jnp.add)
            if do_a:
                s = lax.dot_general(ks[ka, rows, :], qa, _NT, preferred_element_type=F32)
                s_buf[rows, :] = s
                m_acc = jnp.maximum(m_acc, fold8(s, jnp.maximum))
        if do_c:
            o_buf[gc % upb] = o_acc * (1.0 / jnp.sum(l_prev, axis=0, keepdims=True))
        return m_acc, l_acc

    def job_out(g, rows):
        return o_buf[g // pack, rows, tq * (g % pack):tq * (g % pack + 1)]

    def finish_block(qb):
        r0 = qb * tq if isinstance(qb, int) else pl.multiple_of(qb * tq, tq)
        out_rows = pl.ds(r0, tq)
        for c in range(4):
            h = c // 2
            rows = slice(HEAD_DIM * h, HEAD_DIM * (h + 1))
            ot = jnp.concatenate([job_out(2 * c, rows), job_out(2 * c + 1, rows)], axis=0)
            out_ref[out_rows, LANES * c:LANES * (c + 1)] = ot.astype(BF16).T

        dl = dl_ref[...]
        lam = (jnp.exp(jnp.sum(dl[0:1] * dl[1:2], axis=-1, keepdims=True))
               - jnp.exp(jnp.sum(dl[2:3] * dl[3:4], axis=-1, keepdims=True)) + lam_init)

        gain = dg_ref[...] * (1.0 - lam_init)
        for c in range(2):
            parts = []
            for par in range(2):
                g1 = GQA_HEADS + 2 * (2 * c + par)
                rows = slice(HEAD_DIM * par, HEAD_DIM * (par + 1))
                o = job_out(g1, rows) - lam * job_out(g1 + 1, rows)
                parts.append(o * lax.rsqrt(jnp.mean(o * o, axis=0, keepdims=True) + EPS))
            ot = jnp.concatenate(parts, axis=0)
            ot = jnp.concatenate([ot[:, LANES * j:LANES * (j + 1)] * gain for j in range(tq // LANES)], axis=1)
            out_ref[out_rows, GQA_WIDTH + LANES * c:GQA_WIDTH + LANES * (c + 1)] = ot.astype(BF16).T

    bpb = upb // ways
    n_bodies = n_units // ways
    nothing = (None,) * ways

    def body(i, carry, **stages):
        ms, ls = [], []
        for par in range(ways):
            m, l = run(ways * i + par, par, carry[0][par], carry[1][par], **stages)
            ms.append(m)
            ls.append(l)
        if isinstance(i, int):
            if i > 1 and (i - 1) % bpb == 0:
                finish_block((i - 1) // bpb - 1)
        elif n_bodies > bpb:
            @pl.when((i - 1) % bpb == 0)
            def _():
                finish_block((i - 1) // bpb - 1)
        return tuple(ms), tuple(ls)

    carry = body(0, (nothing, nothing), do_b=False, do_c=False)
    carry = body(1, (carry[0], nothing), do_c=False)
    carry = lax.fori_loop(2, n_bodies, body, carry)
    carry = body(n_bodies, carry, do_a=False)
    body(n_bodies + 1, (nothing, carry[1]), do_a=False, do_b=False)


def _attention(qx, k, v, dk, dv, caches, dl, dg, *, layer, seq, nseq, tq, lam_init):
    pack = ATTN_WIDTH // tq
    ways = 2 if seq > ATTN_WIDTH else 4
    n = k.shape[0]
    rows = nseq * seq
    n_ctx = 0 if caches is None else caches[0].shape[2]
    assert n_ctx == 0 or nseq == 1
    t_all = n_ctx + seq
    per_b = lambda bi: (bi, 0)
    in_specs = [pl.BlockSpec((N_JOBS, rows, LANES), lambda bi: (0, bi, 0)),
                pl.BlockSpec((rows, KV_WIDTH), per_b),
                pl.BlockSpec((rows, KV_WIDTH), per_b),
                pl.BlockSpec((rows, DIFF_WIDTH), per_b),
                pl.BlockSpec((rows, DIFF_WIDTH), per_b)]
    args = [qx, k, v, dk, dv]
    if n_ctx:
        cache_idx = lambda bi: (bi, layer, 0, 0)
        in_specs += [pl.BlockSpec((None, None, n_ctx, KV_WIDTH), cache_idx),
                     pl.BlockSpec((None, None, n_ctx, KV_WIDTH), cache_idx),
                     pl.BlockSpec((None, None, n_ctx, DIFF_WIDTH), cache_idx),
                     pl.BlockSpec((None, None, n_ctx, DIFF_WIDTH), cache_idx)]
        args += list(caches)
    in_specs += [pl.BlockSpec((None, 4, DIFF_QK_DIM), lambda bi: (layer, 0, 0)),
                 pl.BlockSpec((None, LANES, LANES), lambda bi: (layer, 0, 0))]
    args += [dl, dg]
    return pl.pallas_call(
        functools.partial(_attn_kernel, n_ctx=n_ctx, seq=seq, nseq=nseq, tq=tq, pack=pack, ways=ways,
                          lam_init=lam_init),
        grid=(n // rows,), in_specs=in_specs,
        out_specs=pl.BlockSpec((rows, D_MODEL - LRU_WIDTH), per_b),
        out_shape=jax.ShapeDtypeStruct((n, D_MODEL - LRU_WIDTH), BF16),
        scratch_shapes=[pltpu.VMEM((3 * nseq, t_all, LANES), BF16), pltpu.VMEM((3 * nseq, LANES, t_all), BF16)]
        + [pltpu.VMEM((t_all, ATTN_WIDTH), F32)] * ways + [pltpu.VMEM((t_all, ATTN_WIDTH), BF16)] * ways
        + [pltpu.VMEM((N_JOBS // pack, LANES, ATTN_WIDTH), F32)],
        compiler_params=_params("arbitrary"), name="attention",
    )(*args)


HALO = 16


def _zero_row(x, r):
    g = r // 8 * 8
    row = lax.broadcasted_iota(jnp.int32, (8, x.shape[1]), 0)
    patched = jnp.where(row == r - g, 0.0, x[g:g + 8])
    return jnp.concatenate([x[:g], patched, x[g + 8:]], axis=0)


def _mix_ffn_kernel(lru_ref, lrup_ref, lrun_ref, att_ref, attp_ref, attn_ref, x_ref, xp_ref, xn_ref,
                    wo_ref, n2_ref, up_hbm, cw_ref, cb_ref, wd_hbm, mod_ref, fin_ref, o_ref,
                    mix, xe, he, act, up_ref, wd_ref, sems, *, layer, seq, tm, final):
    up_copy = pltpu.make_async_copy(up_hbm.at[layer], up_ref, sems.at[0])
    wd_copy = pltpu.make_async_copy(wd_hbm.at[layer], wd_ref, sems.at[1])

    @pl.when(pl.program_id(0) == 0)
    def _():
        up_copy.start()
        wd_copy.start()

    i = pl.program_id(0)
    nts = max(seq // tm, 1)
    pos = i % nts
    ne = tm + 2 * HALO
    for r0, nr, lru_r, att_r, x_r in ((0, HALO, lrup_ref, attp_ref, xp_ref),
                                      (HALO, tm, lru_ref, att_ref, x_ref),
                                      (HALO + tm, HALO, lrun_ref, attn_ref, xn_ref)):
        mix[r0:r0 + nr, 0:LRU_WIDTH] = lru_r[...]
        mix[r0:r0 + nr, LRU_WIDTH:D_MODEL] = att_r[...]
        xe[r0:r0 + nr, :] = x_r[...]
    mod = mod_ref[...]
    x1 = xe[...] + mod[:, 2 * D_MODEL:3 * D_MODEL] * jnp.dot(mix[...], wo_ref[...], preferred_element_type=F32)
    xe[...] = x1
    he[...] = (_rms(x1) * (n2_ref[...] * (1.0 + mod[:, 4 * D_MODEL:5 * D_MODEL]))
               + mod[:, 3 * D_MODEL:4 * D_MODEL]).astype(BF16)

    @pl.when(pos == 0)
    def _():
        he[0:HALO, :] = jnp.zeros((HALO, D_MODEL), BF16)

    @pl.when(pos == nts - 1)
    def _():
        he[HALO + tm:ne, :] = jnp.zeros((HALO, D_MODEL), BF16)

    @pl.when(i == 0)
    def _():
        up_copy.wait()

    def up(c, half, scale):
        cols = slice(half * D_FF + FFN_CHUNK * c, half * D_FF + FFN_CHUNK * (c + 1))
        u = jnp.dot(he[...], up_ref[:, cols], preferred_element_type=F32)
        cw, cb = cw_ref[:, cols], cb_ref[:, cols]
        if scale != 1.0:
            cw, cb = cw * scale, cb * scale
        before = pltpu.roll(u, 1, 0)[HALO:HALO + tm]
        after = pltpu.roll(u, ne - 1, 0)[HALO:HALO + tm]
        for b in range(seq, tm, seq):
            before = _zero_row(before, b)
            after = _zero_row(after, b - 1)
        return cw[0:1] * before + cw[1:2] * u[HALO:HALO + tm] + cw[2:3] * after + cb

    for c in range(N_FFN_CHUNKS):
        a, hg = up(c, 0, 1.0), up(c, 1, 0.5)
        act[:, FFN_CHUNK * c:FFN_CHUNK * (c + 1)] = ((hg * jnp.tanh(hg) + hg) * a).astype(BF16)
    @pl.when(i == 0)
    def _():
        wd_copy.wait()

    down = jnp.dot(act[...], wd_ref[...], preferred_element_type=F32)
    x2 = xe[HALO:HALO + tm, :] + mod[:, 5 * D_MODEL:6 * D_MODEL] * down
    if final:
        x2 = _rms(x2) * fin_ref[...]
    o_ref[...] = x2


def _mix_ffn(lru_o, att_o, x, w_out, n2, up, cw, cb, wd, mod, fin_g, *, layer, seq, tm, mod_row, final):
    n = x.shape[0]
    assert seq % tm == 0 or tm % seq == 0
    nts = max(seq // tm, 1)
    nbh = n // HALO
    row = lambda i: (i, 0)
    prev = lambda i: (jnp.maximum(i * (tm // HALO) - 1, 0), 0)
    nxt = lambda i: (jnp.minimum((i + 1) * (tm // HALO), nbh - 1), 0)
    c2 = lambda i: (0, 0)
    per_layer = lambda i: (layer, 0, 0)
    att_w = D_MODEL - LRU_WIDTH

    def with_halo(width):
        return [pl.BlockSpec((tm, width), row), pl.BlockSpec((HALO, width), prev), pl.BlockSpec((HALO, width), nxt)]

    in_specs = (with_halo(LRU_WIDTH) + with_halo(att_w) + with_halo(D_MODEL)
                + [pl.BlockSpec((None, D_MODEL, D_MODEL), per_layer),
                   pl.BlockSpec((None, 1, D_MODEL), per_layer),
                   pl.BlockSpec(memory_space=pl.ANY),
                   pl.BlockSpec((None, 3, 2 * D_FF), per_layer),
                   pl.BlockSpec((None, 1, 2 * D_FF), per_layer),
                   pl.BlockSpec(memory_space=pl.ANY),
                   pl.BlockSpec((None, 1, 6 * D_MODEL), lambda i: (layer * COND_ROWS + mod_row(i // nts), 0, 0)),
                   pl.BlockSpec((1, D_MODEL), c2)])
    return pl.pallas_call(
        functools.partial(_mix_ffn_kernel, layer=layer, seq=seq, tm=tm, final=final),
        grid=(n // tm,), in_specs=in_specs,
        out_specs=pl.BlockSpec((tm, D_MODEL), row),
        out_shape=jax.ShapeDtypeStruct((n, D_MODEL), F32),
        scratch_shapes=[pltpu.VMEM((tm + 2 * HALO, D_MODEL), BF16), pltpu.VMEM((tm + 2 * HALO, D_MODEL), F32),
                        pltpu.VMEM((tm + 2 * HALO, D_MODEL), BF16), pltpu.VMEM((tm, D_FF), BF16),
                        pltpu.VMEM((D_MODEL, 2 * D_FF), BF16), pltpu.VMEM((D_FF, D_MODEL), BF16),
                        pltpu.SemaphoreType.DMA((2,))],
        compiler_params=_params("arbitrary"), name="mix_ffn",
    )(lru_o, lru_o, lru_o, att_o, att_o, att_o, x, x, x, w_out, n2, up, cw, cb, wd, mod, fin_g)


def _rope_tables(seq, dim):
    rows = seq // GRID_W
    t_row = np.repeat(np.arange(rows, dtype=np.float64), GRID_W)
    t_col = np.tile(np.arange(GRID_W, dtype=np.float64), rows)
    axis_dim = dim // 2
    inv = ROPE_THETA ** (-np.arange(0, axis_dim, 2, dtype=np.float64) / axis_dim)
    ar = t_row[:, None] * inv
    ac = t_col[:, None] * inv
    ang = np.concatenate([ar, ar, ac, ac], axis=-1)
    reps = LANES // dim
    cos = np.tile(np.cos(ang), (1, reps))
    sin = np.tile(np.sin(ang), (1, reps))
    first = (np.arange(LANES) % (dim // 2)) < (dim // 4)
    sin_neg = np.where(first, -sin, 0.0)
    sin_pos = np.where(first, 0.0, sin)
    return tuple(jnp.asarray(t, F32) for t in (cos, sin_neg, sin_pos))


def _gate_weights(w, b):
    nl = w.shape[0]
    eye = jnp.eye(LRU_BLOCKS, dtype=w.dtype)
    wp = w.reshape(nl, 2, LRU_BLOCKS, LRU_BLOCK_W, 2, LRU_BLOCK_W)
    dense = jnp.einsum('lpndqe,nm->lndpqme', wp, eye).reshape(nl, LRU_WIDTH, 4 * LRU_WIDTH)
    bias = jnp.transpose(b.reshape(nl, 2, LRU_BLOCKS, 2, LRU_BLOCK_W), (0, 1, 3, 2, 4)).reshape(nl, 1, 4 * LRU_WIDTH)
    return (0.5 * dense).astype(BF16), 0.5 * bias


def kernel(x_prompt, x_sample, cache_gqa_k, cache_gqa_v, cache_diff_k, cache_diff_v, state_lru, c, c_ctx,
           norm1_g, norm2_g, final_norm_g, ada_w, ada_b, w_in, w_out, lru_conv_w, lru_conv_b, lru_gate_w,
           lru_gate_b, lru_lambda, gqa_q_norm_g, gqa_k_norm_g, diff_lambda, diff_norm_g, ffn_w_up, ffn_conv_w,
           ffn_conv_b, ffn_w_down):
    bp, sp, _ = x_prompt.shape
    bs, ss, _ = x_sample.shape
    n_ctx = cache_gqa_k.shape[2]

    cond = jnp.concatenate([c_ctx[None, :], c, jnp.zeros((COND_ROWS - 1 - bs, D_MODEL), F32)], axis=0)
    mod_all = _ada(cond, ada_w, ada_b)

    caches = (cache_gqa_k.reshape(bs, DEPTH, n_ctx, KV_WIDTH), cache_gqa_v.reshape(bs, DEPTH, n_ctx, KV_WIDTH),
              cache_diff_k.reshape(bs, DEPTH, n_ctx, DIFF_WIDTH), cache_diff_v.reshape(bs, DEPTH, n_ctx, DIFF_WIDTH))
    tables = _rope_tables(ss, HEAD_DIM) + _rope_tables(ss, DIFF_QK_DIM)
    ones_blk = jnp.asarray(np.kron(np.eye(2), np.ones((HEAD_DIM, HEAD_DIM))), BF16)
    zero_state = jnp.zeros((bp, 1, 2, LRU_WIDTH), F32)

    groups = {
        'p': dict(seq=sp, tm=512, tq=256, nseq=4, mod_row=lambda b: 0, tables=None, caches=None),
        's': dict(seq=ss, tm=512, tq=512, nseq=1, mod_row=lambda b: 1 + b, tables=tables, caches=caches),
    }
    xs = {'p': x_prompt.reshape(bp * sp, D_MODEL), 's': x_sample.reshape(bs * ss, D_MODEL)}
    new_k, new_v, new_dk, new_dv, new_st = [], [], [], [], []

    mod = mod_all.reshape(DEPTH * COND_ROWS, 1, 6 * D_MODEL)
    w_in_b = w_in.astype(BF16)
    w_out_b = w_out.astype(BF16)
    up_b = ffn_w_up.astype(BF16)
    wd_b = ffn_w_down.astype(BF16)
    ffn_cb = ffn_conv_b.reshape(DEPTH, 1, 2 * D_FF)
    fin = final_norm_g[None, :]
    gw, gb = _gate_weights(lru_gate_w, lru_gate_b)
    lru_cb = lru_conv_b.reshape(DEPTH, 1, LRU_WIDTH)
    qg = jnp.tile(gqa_q_norm_g, (1, 2)).reshape(DEPTH, 1, LANES)
    kg = jnp.tile(gqa_k_norm_g, (1, 2)).reshape(DEPTH, 1, LANES)
    dg = jnp.broadcast_to(jnp.tile(diff_norm_g, (1, 2))[:, :, None], (DEPTH, LANES, LANES))
    n1 = norm1_g.reshape(DEPTH, 1, D_MODEL)
    n2 = norm2_g.reshape(DEPTH, 1, D_MODEL)

    for l in range(DEPTH):
        lam_init = 0.8 - 0.6 * math.exp(-0.3 * l)
        for name in ('p', 's'):
            g = groups[name]
            seq, tm, mod_row = g['seq'], g['tm'], g['mod_row']
            x = xs[name]
            lru, qx, k, v, dk, dv = _norm_proj(x, mod, n1, w_in_b, qg, kg, ones_blk, g['tables'],
                                               layer=l, seq=seq, tm=1024 if seq % 1024 == 0 else 512,
                                               mod_row=mod_row, kv_dtype=F32 if name == 'p' else BF16)
            h0, state_layer = (zero_state, 0) if name == 'p' else (state_lru, l)
            lru_o, st = _lru(lru, lru_conv_w, lru_cb, gw, gb, lru_lambda, h0,
                             layer=l, state_layer=state_layer, seq=seq, tr=min(seq, 512))
            att_o = _attention(qx, k, v, dk, dv, g['caches'], diff_lambda, dg,
                               layer=l, seq=seq, nseq=g['nseq'], tq=g['tq'], lam_init=lam_init)
            xs[name] = _mix_ffn(lru_o, att_o, x, w_out_b, n2, up_b, ffn_conv_w, ffn_cb, wd_b, mod, fin,
                                layer=l, seq=seq, tm=tm, mod_row=mod_row, final=(l == DEPTH - 1))
            if name == 'p':
                new_k.append(k)
                new_v.append(v)
                new_dk.append(dk)
                new_dv.append(dv)
                new_st.append(st)

    y_prompt = xs['p'].reshape(bp, sp, D_MODEL)
    y_sample = xs['s'].reshape(bs, ss, D_MODEL)
    stack = lambda parts, shape: jnp.stack([p.reshape((bp,) + shape) for p in parts], axis=1)
    return (y_prompt, y_sample,
            stack(new_k, (sp, GQA_KV_HEADS, HEAD_DIM)),
            stack(new_v, (sp, GQA_KV_HEADS, HEAD_DIM)),
            stack(new_dk, (sp, DIFF_HEADS, 2, DIFF_QK_DIM)),
            stack(new_dv, (sp, DIFF_HEADS, HEAD_DIM)),
            jnp.stack(new_st, axis=1))
```
